```python
import math
import jax, jax.numpy as jnp
from jax import lax
import numpy as np

D_MODEL = 2048
BATCH = 4
SEQ = 8192
DEPTH = 4
DEC_BATCH = 16
DEC_SEQ = 32
PAST_LEN = 4096

CHUNK = 64
N_MIXERS = 2
N_SSD = (DEPTH + 1) // 2
N_DIFF = DEPTH // 2
D_INNER = 2 * D_MODEL
SSD_HEAD_DIM = 64
SSD_HEADS = D_INNER // SSD_HEAD_DIM
D_STATE = 128
SSD_GROUPS = 8
CONV_W = 4
CONV_DIM = D_INNER + 2 * SSD_GROUPS * D_STATE
SSD_IN = D_INNER + CONV_DIM + SSD_HEADS
SSD_BLOCK = CHUNK
DIFF_HEADS = 8
DIFF_HEAD_DIM = D_MODEL // DIFF_HEADS // 2
Q_BLOCK = 128
MEM_TOKENS = 256
MEM_HEADS = 4
MEM_HEAD_DIM = D_MODEL // MEM_HEADS
PEER_HEADS = 8
N_KEYS = 128
N_EXPERTS = N_KEYS * N_KEYS
PEER_TOPK = 16
PEER_KEY_DIM = 256
PEER_HALF = PEER_KEY_DIM // 2
PEER_BLOCK = 128
DEEPNORM_ALPHA = (2.0 * DEPTH) ** 0.25
DEEPNORM_BETA = (8.0 * DEPTH) ** -0.25
LN_EPS = 1e-5
RMS_EPS = 1e-5

kernel_name = "hybrid_ssd_diffattn_peer_stream_step"


def post_norm(x, f, g, b):
    h = (DEEPNORM_ALPHA * x + f).astype(jnp.float32)
    mu = jnp.mean(h, axis=-1, keepdims=True)
    var = jnp.mean(jnp.square(h - mu), axis=-1, keepdims=True)
    return ((h - mu) * lax.rsqrt(var + LN_EPS)).astype(x.dtype) * g + b


def rmsnorm(x, g):
    xf = x.astype(jnp.float32)
    return (xf * lax.rsqrt(jnp.mean(xf * xf, axis=-1, keepdims=True) + RMS_EPS)).astype(x.dtype) * g


def group_rmsnorm(y, g):
    b, l, d = y.shape
    yf = y.astype(jnp.float32).reshape(b, l, SSD_GROUPS, d // SSD_GROUPS)
    yf = yf * lax.rsqrt(jnp.mean(yf * yf, axis=-1, keepdims=True) + RMS_EPS)
    return yf.reshape(b, l, d).astype(y.dtype) * g


def causal_conv(xbc, prev, w, bias):
    l = xbc.shape[1]
    xp = jnp.concatenate([prev.astype(xbc.dtype), xbc], axis=1)
    out = bias + w[0] * xp[:, 0:l]
    for k in range(1, CONV_W):
        out = out + w[k] * xp[:, k:k + l]
    return out, xp[:, l:]


def ssd_scan(x, dt, a, bm, cm, h0):
    f32 = jnp.float32
    b, l, h, p = x.shape
    g, n = bm.shape[2], bm.shape[3]
    hg = h // g
    nc = l // SSD_BLOCK
    dt = dt.astype(f32)
    da = dt * a
    xdt = x.astype(f32) * dt[..., None]

    def blocks(t):
        return jnp.moveaxis(t.reshape((b, nc, SSD_BLOCK) + t.shape[2:]), 1, 0)

    tri = jnp.tril(jnp.ones((SSD_BLOCK, SSD_BLOCK), dtype=bool))[None, :, :, None]

    def step(hst, inp):
        xc, dac, bc, cc = inp
        xg = xc.reshape(b, SSD_BLOCK, g, hg, p)
        cum = jnp.cumsum(dac, axis=1)
        decay = jnp.exp(jnp.where(tri, cum[:, :, None, :] - cum[:, None, :, :], -jnp.inf))
        decay = decay.reshape(b, SSD_BLOCK, SSD_BLOCK, g, hg)
        cb = jnp.einsum('blgn,bsgn->bgls', cc, bc)
        y_diag = jnp.einsum('bgls,blsgh,bsghp->blghp', cb, decay, xg)
        y_off = jnp.einsum('blgn,bghpn,blgh->blghp', cc, hst, jnp.exp(cum).reshape(b, SSD_BLOCK, g, hg))
        to_end = jnp.exp(cum[:, -1:] - cum).reshape(b, SSD_BLOCK, g, hg)
        h_new = hst * jnp.exp(cum[:, -1]).reshape(b, g, hg)[..., None, None] + jnp.einsum(
            'bsgn,bsgh,bsghp->bghpn', bc, to_end, xg)
        return h_new, (y_diag + y_off).reshape(b, SSD_BLOCK, h, p)

    h_fin, ys = lax.scan(step, h0.astype(f32).reshape(b, g, hg, p, n),
                         (blocks(xdt), blocks(da), blocks(bm.astype(f32)), blocks(cm.astype(f32))))
    y = jnp.moveaxis(ys, 0, 1).reshape(b, l, h, p)
    return y, h_fin.reshape(b, h, p, n).astype(h0.dtype)


def ssd_mixer(x, conv_prev, h0, w_in, conv_w, conv_b, a_log, dt_bias, d_skip, norm_g, w_out):
    b, l, _ = x.shape
    proj = x @ w_in
    z = proj[..., :D_INNER]
    xbc = proj[..., D_INNER:D_INNER + CONV_DIM]
    dt_raw = proj[..., D_INNER + CONV_DIM:]
    xbc, conv_new = causal_conv(xbc, conv_prev, conv_w, conv_b)
    xbc = jax.nn.silu(xbc)
    xs = xbc[..., :D_INNER].reshape(b, l, SSD_HEADS, SSD_HEAD_DIM)
    bm = xbc[..., D_INNER:D_INNER + SSD_GROUPS * D_STATE].reshape(b, l, SSD_GROUPS, D_STATE)
    cm = xbc[..., D_INNER + SSD_GROUPS * D_STATE:].reshape(b, l, SSD_GROUPS, D_STATE)
    dt = jax.nn.softplus((dt_raw + dt_bias).astype(jnp.float32))
    a = -jnp.exp(a_log.astype(jnp.float32))
    lp = -(-l // SSD_BLOCK) * SSD_BLOCK
    pad4 = ((0, 0), (0, lp - l), (0, 0), (0, 0))
    y, h_new = ssd_scan(jnp.pad(xs, pad4), jnp.pad(dt, pad4[:3]), a,
                        jnp.pad(bm, pad4), jnp.pad(cm, pad4), h0)
    y = y[:, :l].astype(x.dtype) + xs * d_skip[:, None]
    y = y.reshape(b, l, D_INNER) * jax.nn.silu(z)
    y = group_rmsnorm(y, norm_g)
    return y @ w_out, conv_new, h_new


def diff_lambda(lam_vecs, layer_idx):
    lam_init = 0.8 - 0.6 * math.exp(-0.3 * layer_idx)
    lv = lam_vecs.astype(jnp.float32)
    lam = jnp.exp(jnp.sum(lv[0] * lv[1])) - jnp.exp(jnp.sum(lv[2] * lv[3])) + lam_init
    return lam, lam_init


def diff_qkv(x, w_qkv):
    b, l, _ = x.shape
    qkv = x @ w_qkv
    q = qkv[..., :D_MODEL].reshape(b, l, DIFF_HEADS, 2, DIFF_HEAD_DIM)
    k = qkv[..., D_MODEL:2 * D_MODEL].reshape(b, l, DIFF_HEADS, 2, DIFF_HEAD_DIM)
    v = qkv[..., 2 * D_MODEL:].reshape(b, l, DIFF_HEADS, 2 * DIFF_HEAD_DIM)
    return q, k, v


def diff_core(q, k, v, mask, lam, lam_init, subln_g):
    s = jnp.einsum('bqhjd,bkhjd->bhjqk', q, k).astype(jnp.float32) * (DIFF_HEAD_DIM ** -0.5)
    if mask is not None:
        s = jnp.where(mask, s, -jnp.inf)
    p = jax.nn.softmax(s, axis=-1)
    p = (p[:, :, 0] - lam * p[:, :, 1]).astype(v.dtype)
    o = jnp.einsum('bhqk,bkhe->bqhe', p, v)
    return rmsnorm(o, subln_g) * (1.0 - lam_init)


def diff_prompt(x, w_qkv, subln_g, w_o, lam, lam_init):
    b, l, _ = x.shape
    q, k, v = diff_qkv(x, w_qkv)
    nb = l // Q_BLOCK
    kchunk = jnp.arange(l) // CHUNK

    def block(args):
        qb, bi = args
        qchunk = (bi * Q_BLOCK + jnp.arange(Q_BLOCK)) // CHUNK
        mask = kchunk[None, :] <= qchunk[:, None]
        return diff_core(qb, k, v, mask, lam, lam_init, subln_g)

    qbs = jnp.moveaxis(q.reshape(b, nb, Q_BLOCK, DIFF_HEADS, 2, DIFF_HEAD_DIM), 1, 0)
    o = lax.map(block, (qbs, jnp.arange(nb)))
    o = jnp.moveaxis(o, 0, 1).reshape(b, l, D_MODEL)
    return o @ w_o, k, v


def diff_sample(x, k_cache, v_cache, w_qkv, subln_g, w_o, lam, lam_init):
    b, l, _ = x.shape
    q, k, v = diff_qkv(x, w_qkv)
    k_all = jnp.concatenate([k_cache.astype(k.dtype), k], axis=1)
    v_all = jnp.concatenate([v_cache.astype(v.dtype), v], axis=1)
    o = diff_core(q, k_all, v_all, None, lam, lam_init, subln_g)
    return o.reshape(b, l, D_MODEL) @ w_o, k, v


def mem_kv(mem, w_kv):
    b, m, _ = mem.shape
    kv = mem @ w_kv
    return (kv[..., :D_MODEL].reshape(b, m, MEM_HEADS, MEM_HEAD_DIM),
            kv[..., D_MODEL:].reshape(b, m, MEM_HEADS, MEM_HEAD_DIM))


def mem_attn(x, mk, mv, w_q, w_o):
    b, l, _ = x.shape
    q = (x @ w_q).reshape(b, l, MEM_HEADS, MEM_HEAD_DIM)
    s = jnp.einsum('bqhd,bmhd->bhqm', q, mk.astype(q.dtype)).astype(jnp.float32) * (MEM_HEAD_DIM ** -0.5)
    p = jax.nn.softmax(s, axis=-1).astype(x.dtype)
    o = jnp.einsum('bhqm,bmhd->bqhd', p, mv.astype(x.dtype)).reshape(b, l, D_MODEL)
    return o @ w_o


def peer(x, w_q, sub_keys, u_tab, v_tab):
    b, l, d = x.shape
    n = b * l
    nb = -(-n // PEER_BLOCK)
    t = jnp.pad(x.reshape(n, d), ((0, nb * PEER_BLOCK - n), (0, 0))).reshape(nb, PEER_BLOCK, d)

    def block(tb):
        q = (tb @ w_q).reshape(PEER_BLOCK, PEER_HEADS, 2, PEER_HALF)
        s = jnp.einsum('thjc,hjkc->thjk', q, sub_keys)
        sv, si = lax.top_k(s, PEER_TOPK)
        comb = (sv[:, :, 0, :, None] + sv[:, :, 1, None, :]).reshape(PEER_BLOCK, PEER_HEADS, PEER_TOPK * PEER_TOPK)
        cand = (si[:, :, 0, :, None] * N_KEYS + si[:, :, 1, None, :]).reshape(PEER_BLOCK, PEER_HEADS, PEER_TOPK * PEER_TOPK)
        top_s, top_i = lax.top_k(comb, PEER_TOPK)
        experts = jnp.take_along_axis(cand, top_i, axis=-1)
        gate = jax.nn.softmax(top_s.astype(jnp.float32), axis=-1).astype(tb.dtype)
        act = jax.nn.gelu(jnp.einsum('thkd,td->thk', u_tab[experts], tb))
        return jnp.einsum('thk,thkd->td', gate * act, v_tab[experts])

    out = lax.map(block, t).reshape(nb * PEER_BLOCK, d)[:n]
    return out.reshape(b, l, d)


def setup_inputs(seed: int = 0) -> dict:
    key = jax.random.key(seed)
    ks = iter(jax.random.split(key, 40))
    f32 = jnp.float32

    def nrm(shape, scale=1.0):
        return jax.random.normal(next(ks), shape, f32) * scale

    dt0 = jnp.exp(jax.random.uniform(next(ks), (N_SSD, SSD_HEADS), f32) * (math.log(0.1) - math.log(0.001)) + math.log(0.001))
    return {
        "x_prompt": nrm((BATCH, SEQ, D_MODEL)),
        "x_sample": nrm((DEC_BATCH, DEC_SEQ, D_MODEL)),
        "state_ssd_conv": nrm((N_SSD, DEC_BATCH, CONV_W - 1, CONV_DIM)),
        "state_ssd_h": nrm((N_SSD, DEC_BATCH, SSD_HEADS, SSD_HEAD_DIM, D_STATE), 0.5),
        "cache_diff_k": nrm((N_DIFF, DEC_BATCH, PAST_LEN, DIFF_HEADS, 2, DIFF_HEAD_DIM)),
        "cache_diff_v": nrm((N_DIFF, DEC_BATCH, PAST_LEN, DIFF_HEADS, 2 * DIFF_HEAD_DIM)),
        "cache_mem_k": nrm((DEPTH, DEC_BATCH, MEM_TOKENS, MEM_HEADS, MEM_HEAD_DIM)),
        "cache_mem_v": nrm((DEPTH, DEC_BATCH, MEM_TOKENS, MEM_HEADS, MEM_HEAD_DIM)),
        "mem_prompt": nrm((BATCH, MEM_TOKENS, D_MODEL)),
        "ln_g": 1.0 + nrm((DEPTH, 3, D_MODEL), 0.01),
        "ln_b": nrm((DEPTH, 3, D_MODEL), 0.01),
        "ssd_w_in": nrm((N_SSD, D_MODEL, SSD_IN), D_MODEL ** -0.5),
        "ssd_conv_w": nrm((N_SSD, CONV_W, CONV_DIM), CONV_W ** -0.5),
        "ssd_conv_b": nrm((N_SSD, CONV_DIM), 0.01),
        "ssd_a_log": jnp.log(jax.random.uniform(next(ks), (N_SSD, SSD_HEADS), f32, 1.0, 16.0)),
        "ssd_dt_bias": dt0 + jnp.log(-jnp.expm1(-dt0)),
        "ssd_d": 1.0 + nrm((N_SSD, SSD_HEADS), 0.01),
        "ssd_norm_g": 1.0 + nrm((N_SSD, D_INNER), 0.01),
        "ssd_w_out": nrm((N_SSD, D_INNER, D_MODEL), D_INNER ** -0.5 * DEEPNORM_BETA),
        "diff_w_qkv": nrm((N_DIFF, D_MODEL, 3 * D_MODEL), D_MODEL ** -0.5),
        "diff_lam": nrm((N_DIFF, 4, DIFF_HEAD_DIM), 0.1),
        "diff_subln_g": 1.0 + nrm((N_DIFF, 2 * DIFF_HEAD_DIM), 0.01),
        "diff_w_o": nrm((N_DIFF, D_MODEL, D_MODEL), D_MODEL ** -0.5 * DEEPNORM_BETA),
        "mem_w_q": nrm((DEPTH, D_MODEL, D_MODEL), D_MODEL ** -0.5),
        "mem_w_kv": nrm((DEPTH, D_MODEL, 2 * D_MODEL), D_MODEL ** -0.5),
        "mem_w_o": nrm((DEPTH, D_MODEL, D_MODEL), D_MODEL ** -0.5 * DEEPNORM_BETA),
        "peer_w_q": nrm((DEPTH, D_MODEL, PEER_HEADS * PEER_KEY_DIM), D_MODEL ** -0.5),
        "peer_keys": nrm((DEPTH, PEER_HEADS, 2, N_KEYS, PEER_HALF), PEER_HALF ** -0.5),
        "peer_u": nrm((DEPTH, N_EXPERTS, D_MODEL), D_MODEL ** -0.5),
        "peer_v": nrm((DEPTH, N_EXPERTS, D_MODEL), D_MODEL ** -0.5 * DEEPNORM_BETA),
    }


def reference(x_prompt, x_sample, state_ssd_conv, state_ssd_h, cache_diff_k, cache_diff_v,
              cache_mem_k, cache_mem_v, mem_prompt, ln_g, ln_b,
              ssd_w_in, ssd_conv_w, ssd_conv_b, ssd_a_log, ssd_dt_bias, ssd_d, ssd_norm_g, ssd_w_out,
              diff_w_qkv, diff_lam, diff_subln_g, diff_w_o,
              mem_w_q, mem_w_kv, mem_w_o, peer_w_q, peer_keys, peer_u, peer_v):
    yp, ys = x_prompt, x_sample
    conv_p, h_p, conv_s, h_s = [], [], [], []
    dk_p, dv_p, dk_s, dv_s = [], [], [], []
    mk_p, mv_p = [], []
    for i in range(DEPTH):
        j = i // N_MIXERS
        if i % N_MIXERS == 0:
            sw = (ssd_w_in[j], ssd_conv_w[j], ssd_conv_b[j], ssd_a_log[j], ssd_dt_bias[j],
                  ssd_d[j], ssd_norm_g[j], ssd_w_out[j])
            bp = yp.shape[0]
            mix_p, c_new, h_new = ssd_mixer(
                yp, jnp.zeros((bp, CONV_W - 1, CONV_DIM), yp.dtype),
                jnp.zeros((bp, SSD_HEADS, SSD_HEAD_DIM, D_STATE), yp.dtype), *sw)
            mix_s, c_new_s, h_new_s = ssd_mixer(ys, state_ssd_conv[j], state_ssd_h[j], *sw)
            conv_p.append(c_new)
            h_p.append(h_new)
            conv_s.append(c_new_s)
            h_s.append(h_new_s)
        else:
            lam, lam_init = diff_lambda(diff_lam[j], i)
            dw = (diff_w_qkv[j], diff_subln_g[j], diff_w_o[j], lam, lam_init)
            mix_p, k_new, v_new = diff_prompt(yp, *dw)
            mix_s, k_new_s, v_new_s = diff_sample(ys, cache_diff_k[j], cache_diff_v[j], *dw)
            dk_p.append(k_new)
            dv_p.append(v_new)
            dk_s.append(k_new_s)
            dv_s.append(v_new_s)
        yp = post_norm(yp, mix_p, ln_g[i, 0], ln_b[i, 0])
        ys = post_norm(ys, mix_s, ln_g[i, 0], ln_b[i, 0])
        mk, mv = mem_kv(mem_prompt, mem_w_kv[i])
        mk_p.append(mk)
        mv_p.append(mv)
        yp = post_norm(yp, mem_attn(yp, mk, mv, mem_w_q[i], mem_w_o[i]), ln_g[i, 1], ln_b[i, 1])
        ys = post_norm(ys, mem_attn(ys, cache_mem_k[i], cache_mem_v[i], mem_w_q[i], mem_w_o[i]), ln_g[i, 1], ln_b[i, 1])
        yp = post_norm(yp, peer(yp, peer_w_q[i], peer_keys[i], peer_u[i], peer_v[i]), ln_g[i, 2], ln_b[i, 2])
        ys = post_norm(ys, peer(ys, peer_w_q[i], peer_keys[i], peer_u[i], peer_v[i]), ln_g[i, 2], ln_b[i, 2])
    return (yp, ys, jnp.stack(conv_p), jnp.stack(h_p), jnp.stack(dk_p), jnp.stack(dv_p),
            jnp.stack(mk_p), jnp.stack(mv_p), jnp.stack(conv_s), jnp.stack(h_s),
            jnp.stack(dk_s), jnp.stack(dv_s))
```

```python
import functools
import math

import jax
import jax.numpy as jnp
from jax import lax
from jax.experimental import pallas as pl
from jax.experimental.pallas import tpu as pltpu

F32 = jnp.float32
BF16 = jnp.bfloat16

LANES = 128
SUBLANES = 8
VMEM_LIMIT_BYTES = 56 * 1024 * 1024

MASK_CHUNK = 64
CONV_W = 4
PEER_TOPK = 16
LN_EPS = 1e-5
RMS_EPS = 1e-5
SSD_CHUNK = 128

TM = 512
TN = 512
TK = 512
TQ_DIFF = 512
TK_DIFF = 512
TQ_MEM = 512
T_PEER = 512
EC_PEER = 512
TR_CONV = 512
TC_CONV = 512

_NT = (((1,), (1,)), ((), ()))


def _tile(n, target, mult):
    t = (min(target, n) // mult) * mult
    while t >= mult:
        if n % t == 0:
            return t
        t -= mult
    return n


def _cp(*sem):
    return pltpu.CompilerParams(dimension_semantics=sem, vmem_limit_bytes=VMEM_LIMIT_BYTES)


def _dot(a, b):
    return jnp.dot(a, b, preferred_element_type=F32)


def _dot_nt(a, b):
    return lax.dot_general(a, b, _NT, preferred_element_type=F32)


def _layer_norm(h, g, b):
    mu = jnp.mean(h, axis=-1, keepdims=True)
    d = h - mu
    var = jnp.mean(d * d, axis=-1, keepdims=True)
    return d * lax.rsqrt(var + LN_EPS) * g + b


def _mm_kernel(a_ref, w_ref, *o_refs):
    acc = _dot(a_ref[...].astype(BF16), w_ref[...])
    for o in o_refs:
        o[...] = acc.astype(o.dtype)


def matmul(a, w, out_dtypes=(F32,)):
    n, k = a.shape
    m = w.shape[1]
    tm, tn = _tile(n, TM, 16), _tile(m, TN, LANES)
    outs = pl.pallas_call(
        _mm_kernel,
        grid=(n // tm, m // tn),
        in_specs=[pl.BlockSpec((tm, k), lambda i, j: (i, 0)),
                  pl.BlockSpec((k, tn), lambda i, j: (0, j))],
        out_specs=[pl.BlockSpec((tm, tn), lambda i, j: (i, j)) for _ in out_dtypes],
        out_shape=[jax.ShapeDtypeStruct((n, m), d) for d in out_dtypes],
        compiler_params=_cp("parallel", "parallel"),
        name="matmul",
    )(a, w)
    return outs[0] if len(out_dtypes) == 1 else outs


def _mm_ln_kernel(a_ref, w_ref, x_ref, g_ref, b_ref, *refs, alpha, nk, with_t):
    o_ref, acc_ref = refs[0], refs[-1]
    k = pl.program_id(1)

    @pl.when(k == 0)
    def _():
        acc_ref[...] = jnp.zeros_like(acc_ref)

    acc_ref[...] += _dot(a_ref[...].astype(BF16), w_ref[...])

    @pl.when(k == nk - 1)
    def _():
        y = _layer_norm(alpha * x_ref[...] + acc_ref[...], g_ref[...], b_ref[...])
        o_ref[...] = y
        if with_t:
            refs[1][...] = y.T.astype(BF16)


def matmul_ln(a, w, x, g, b, alpha, with_t=False):
    n, kdim = a.shape
    d = w.shape[1]
    tm, tk = _tile(n, TM, LANES if with_t else 16), _tile(kdim, TK, LANES)
    nk = kdim // tk
    out_shape = [jax.ShapeDtypeStruct((n, d), F32)]
    out_specs = [pl.BlockSpec((tm, d), lambda i, k: (i, 0))]
    if with_t:
        out_shape.append(jax.ShapeDtypeStruct((d, n), BF16))
        out_specs.append(pl.BlockSpec((d, tm), lambda i, k: (0, i)))
    outs = pl.pallas_call(
        functools.partial(_mm_ln_kernel, alpha=alpha, nk=nk, with_t=with_t),
        grid=(n // tm, nk),
        in_specs=[pl.BlockSpec((tm, tk), lambda i, k: (i, k)),
                  pl.BlockSpec((tk, d), lambda i, k: (k, 0)),
                  pl.BlockSpec((tm, d), lambda i, k: (i, 0)),
                  pl.BlockSpec((1, d), lambda i, k: (0, 0)),
                  pl.BlockSpec((1, d), lambda i, k: (0, 0))],
        out_specs=out_specs,
        out_shape=out_shape,
        scratch_shapes=[pltpu.VMEM((tm, d), F32)],
        compiler_params=_cp("parallel", "arbitrary"),
        name="matmul_ln",
    )(a, w, x, g.reshape(1, d), b.reshape(1, d))
    return outs if with_t else outs[0]


def _post_norm_kernel(x_ref, f_ref, g_ref, b_ref, o_ref, *, alpha):
    o_ref[...] = _layer_norm(alpha * x_ref[...] + f_ref[...], g_ref[...], b_ref[...])


def post_norm(x, f, g, b, alpha):
    n, d = x.shape
    tm = _tile(n, TM, SUBLANES)
    row = pl.BlockSpec((tm, d), lambda i: (i, 0))
    vec = pl.BlockSpec((1, d), lambda i: (0, 0))
    return pl.pallas_call(
        functools.partial(_post_norm_kernel, alpha=alpha),
        grid=(n // tm,),
        in_specs=[row, row, vec, vec],
        out_specs=row,
        out_shape=jax.ShapeDtypeStruct((n, d), F32),
        compiler_params=_cp("parallel"),
        name="post_norm",
    )(x, f, g.reshape(1, d), b.reshape(1, d))


def _conv_kernel(x_ref, halo_ref, st_ref, w_ref, b_ref, o_ref, *, tr):
    r = pl.program_id(1)
    x = x_ref[...]
    prev = jnp.where(r == 0, st_ref[0], halo_ref[...])
    ext = jnp.concatenate([prev, x], axis=0)
    w = w_ref[...]
    acc = b_ref[...] + w[0:1] * ext[5:5 + tr]
    acc = acc + w[1:2] * ext[6:6 + tr]
    acc = acc + w[2:3] * ext[7:7 + tr]
    acc = acc + w[3:4] * x
    o_ref[...] = acc * jax.nn.sigmoid(acc)


def conv_silu(xbc, state8, w, bias, n_seq, seq):
    n, c = xbc.shape
    tr, tc = _tile(seq, TR_CONV, SUBLANES), _tile(c, TC_CONV, LANES)
    rb = seq // tr

    def halo_map(s, r, j):
        return (jnp.maximum((s * seq + r * tr) // SUBLANES - 1, 0), j)

    return pl.pallas_call(
        functools.partial(_conv_kernel, tr=tr),
        grid=(n_seq, rb, c // tc),
        in_specs=[pl.BlockSpec((tr, tc), lambda s, r, j: (s * rb + r, j)),
                  pl.BlockSpec((SUBLANES, tc), halo_map),
                  pl.BlockSpec((1, SUBLANES, tc), lambda s, r, j: (s, 0, j)),
                  pl.BlockSpec((CONV_W, tc), lambda s, r, j: (0, j)),
                  pl.BlockSpec((1, tc), lambda s, r, j: (0, j))],
        out_specs=pl.BlockSpec((tr, tc), lambda s, r, j: (s * rb + r, j)),
        out_shape=jax.ShapeDtypeStruct((n, c), F32),
        compiler_params=_cp("parallel", "parallel", "parallel"),
        name="conv_silu",
    )(xbc, xbc, state8, w, bias.reshape(1, c))


def _ssd_kernel(x_ref, b_ref, c_ref, z_ref, dt_ref, dtb_ref, a_ref, d_ref, ng_ref, h0_ref,
                y_ref, hout_ref, ht_ref, *, rows, valid, nc, hpg, hd):
    c = pl.program_id(2)
    gw = hpg * hd

    @pl.when(c == 0)
    def _():
        ht_ref[...] = h0_ref[0].reshape(gw, h0_ref.shape[-1]).T

    def pad(v):
        if valid == rows:
            return v
        return jnp.concatenate([v, jnp.zeros((rows - valid, v.shape[1]), v.dtype)], axis=0)

    x = pad(x_ref[...])
    bm = pad(b_ref[...])
    cm = pad(c_ref[...])
    dt = pad(jax.nn.softplus(dt_ref[...] + dtb_ref[...]))
    da = dt * a_ref[...]
    ri = lax.broadcasted_iota(jnp.int32, (rows, rows), 0)
    ci = lax.broadcasted_iota(jnp.int32, (rows, rows), 1)
    tri = ri >= ci
    cum = jnp.dot(tri.astype(F32), da, precision=lax.Precision.HIGHEST,
                  preferred_element_type=F32)
    cum_t = cum.T
    ecum = jnp.exp(cum)
    tot = cum[rows - 1:rows, :]
    toend = jnp.exp(tot - cum)
    etot = jnp.exp(tot)

    cb16 = cm.astype(BF16)
    cb = _dot_nt(cb16, bm.astype(BF16))
    ht = ht_ref[...]
    yoff = _dot(cb16, ht.astype(BF16))
    ys, xws, ets = [], [], []
    for h in range(hpg):
        sl = slice(h * hd, (h + 1) * hd)
        xdt = x[:, sl] * dt[:, h:h + 1]
        dec = jnp.exp(jnp.where(tri, cum[:, h:h + 1] - cum_t[h:h + 1, :], -jnp.inf))
        yd = _dot((cb * dec).astype(BF16), xdt.astype(BF16))
        ys.append(yd + yoff[:, sl] * ecum[:, h:h + 1])
        xws.append(xdt * toend[:, h:h + 1])
        ets.append(jnp.broadcast_to(etot[:, h:h + 1], (1, hd)))
    y = jnp.concatenate(ys, axis=1)
    xw = jnp.concatenate(xws, axis=1)
    ht_new = ht * jnp.concatenate(ets, axis=1) + _dot(bm.T.astype(BF16), xw.astype(BF16))
    ht_ref[...] = ht_new

    z = pad(z_ref[...])
    y = (y + x * d_ref[...]) * (z * jax.nn.sigmoid(z))
    ms = jnp.mean(y * y, axis=-1, keepdims=True)
    yn = y * lax.rsqrt(ms + RMS_EPS) * ng_ref[...]
    y_ref[...] = yn[:valid].astype(y_ref.dtype)

    @pl.when(c == nc - 1)
    def _():
        hout_ref[0] = ht_new.T.reshape(hout_ref.shape[1:])


def ssd_scan(xbc, z, dt_raw, dtb, a, d_e, ng, h0, n_seq, seq, d_inner, d_state, groups):
    n = xbc.shape[0]
    heads, hd = h0.shape[1], h0.shape[2]
    hpg = heads // groups
    gw = d_inner // groups
    assert gw == hpg * hd and d_state == LANES and gw % LANES == 0
    rows = SSD_CHUNK
    valid = rows if seq % rows == 0 else seq
    assert valid == rows or seq < rows
    nc = seq // valid
    boff, coff = d_inner // d_state, d_inner // d_state + groups
    row = lambda s, g, c: s * nc + c
    y, h_new = pl.pallas_call(
        functools.partial(_ssd_kernel, rows=rows, valid=valid, nc=nc, hpg=hpg, hd=hd),
        grid=(n_seq, groups, nc),
        in_specs=[pl.BlockSpec((valid, gw), lambda s, g, c: (row(s, g, c), g)),
                  pl.BlockSpec((valid, d_state), lambda s, g, c: (row(s, g, c), boff + g)),
                  pl.BlockSpec((valid, d_state), lambda s, g, c: (row(s, g, c), coff + g)),
                  pl.BlockSpec((valid, gw), lambda s, g, c: (row(s, g, c), g)),
                  pl.BlockSpec((valid, LANES), lambda s, g, c: (row(s, g, c), g)),
                  pl.BlockSpec((1, LANES), lambda s, g, c: (0, g)),
                  pl.BlockSpec((1, LANES), lambda s, g, c: (0, g)),
                  pl.BlockSpec((1, gw), lambda s, g, c: (0, g)),
                  pl.BlockSpec((1, gw), lambda s, g, c: (0, g)),
                  pl.BlockSpec((1, hpg, hd, d_state), lambda s, g, c: (s, g, 0, 0))],
        out_specs=[pl.BlockSpec((valid, gw), lambda s, g, c: (row(s, g, c), g)),
                   pl.BlockSpec((1, hpg, hd, d_state), lambda s, g, c: (s, g, 0, 0))],
        out_shape=[jax.ShapeDtypeStruct((n, d_inner), BF16),
                   jax.ShapeDtypeStruct(h0.shape, F32)],
        scratch_shapes=[pltpu.VMEM((d_state, gw), F32)],
        compiler_params=_cp("parallel", "parallel", "arbitrary"),
        name="ssd_scan",
    )(xbc, xbc, xbc, z, dt_raw, dtb, a, d_e, ng, h0)
    return y, h_new


def _softmax_step(s, m_ref, l_ref, acc_ref, vb):
    m_old = m_ref[...]
    m_new = jnp.maximum(m_old, jnp.max(s, axis=-1, keepdims=True))
    p = jnp.exp(s - m_new)
    corr = jnp.exp(m_old - m_new)
    l_ref[...] = corr * l_ref[...] + jnp.sum(p, axis=-1, keepdims=True)
    acc_ref[...] = corr * acc_ref[...] + _dot(p.astype(BF16), vb)
    m_ref[...] = m_new


def _diff_finish(lam, lam_init, g_ref, o_ref, l0, a0, l1, a1):
    o = a0[...] / l0[...] - lam * (a1[...] / l1[...])
    ms = jnp.mean(o * o, axis=-1, keepdims=True)
    o_ref[...] = (o * lax.rsqrt(ms + RMS_EPS) * g_ref[...] * (1.0 - lam_init)).astype(o_ref.dtype)


def _diff_init(m0, l0, a0, m1, l1, a1):
    for m, l, a in ((m0, l0, a0), (m1, l1, a1)):
        m[...] = jnp.full_like(m, -jnp.inf)
        l[...] = jnp.zeros_like(l)
        a[...] = jnp.zeros_like(a)


def _diff_prompt_kernel(lam_ref, q_ref, k_ref, v_ref, g_ref, o_ref, m0, l0, a0, m1, l1, a1,
                        *, tq, tk, hd, scale, lam_init):
    qi = pl.program_id(2)
    _diff_init(m0, l0, a0, m1, l1, a1)
    q = q_ref[...]
    q0, q1 = q[:, :hd], q[:, hd:]
    per_q = tq // tk

    def block(ki, mask):
        off = pl.multiple_of(ki * tk, tk)
        kb = k_ref[pl.ds(off, tk), :]
        vb = v_ref[pl.ds(off, tk), :]
        for qj, kj, m, l, a in ((q0, kb[:, :hd], m0, l0, a0), (q1, kb[:, hd:], m1, l1, a1)):
            s = _dot_nt(qj, kj) * scale
            if mask is not None:
                s = jnp.where(mask, s, -jnp.inf)
            _softmax_step(s, m, l, a, vb)

    def body(ki, carry):
        block(ki, None)
        return carry

    lax.fori_loop(0, qi * per_q, body, 0)
    rc = lax.broadcasted_iota(jnp.int32, (tq, tk), 0) // MASK_CHUNK
    cc = lax.broadcasted_iota(jnp.int32, (tq, tk), 1) // MASK_CHUNK
    for d in range(per_q):
        block(qi * per_q + d, cc + d * (tk // MASK_CHUNK) <= rc)
    _diff_finish(lam_ref[0], lam_init, g_ref, o_ref, l0, a0, l1, a1)


def diff_attn_prompt(q, k, v, lam, g, n_seq, seq, heads, lam_init):
    n, d = q.shape
    hw = d // heads
    hd = hw // 2
    tq = _tile(seq, TQ_DIFF, MASK_CHUNK)
    tk = _tile(tq, TK_DIFF, MASK_CHUNK)
    qb = seq // tq
    stat = pltpu.VMEM((tq, 1), F32)
    acc = pltpu.VMEM((tq, hw), F32)
    return pl.pallas_call(
        functools.partial(_diff_prompt_kernel, tq=tq, tk=tk, hd=hd, scale=hd ** -0.5, lam_init=lam_init),
        grid=(n_seq, heads, qb),
        in_specs=[pl.BlockSpec(memory_space=pltpu.SMEM),
                  pl.BlockSpec((tq, hw), lambda b, h, i: (b * qb + i, h)),
                  pl.BlockSpec((seq, hw), lambda b, h, i: (b, h)),
                  pl.BlockSpec((seq, hw), lambda b, h, i: (b, h)),
                  pl.BlockSpec((1, hw), lambda b, h, i: (0, 0))],
        out_specs=pl.BlockSpec((tq, hw), lambda b, h, i: (b * qb + i, h)),
        out_shape=jax.ShapeDtypeStruct((n, d), BF16),
        scratch_shapes=[stat, stat, acc, stat, stat, acc],
        compiler_params=_cp("parallel", "parallel", "arbitrary"),
        name="diff_attn_prompt",
    )(lam, q, k, v, g.reshape(1, hw))


def _diff_sample_kernel(lam_ref, q_ref, kc_ref, vc_ref, kn_ref, vn_ref, g_ref, o_ref,
                        m0, l0, a0, m1, l1, a1, *, tk, nkb, hd, scale, lam_init):
    _diff_init(m0, l0, a0, m1, l1, a1)
    q = q_ref[...]
    q0, q1 = q[:, :hd], q[:, hd:]

    def step(kb, vb):
        for qj, kj, m, l, a in ((q0, kb[:, :hd], m0, l0, a0), (q1, kb[:, hd:], m1, l1, a1)):
            _softmax_step(_dot_nt(qj, kj) * scale, m, l, a, vb)

    def body(ki, carry):
        off = pl.multiple_of(ki * tk, tk)
        step(kc_ref[0, pl.ds(off, tk), :].astype(BF16), vc_ref[0, pl.ds(off, tk), :].astype(BF16))
        return carry

    lax.fori_loop(0, nkb, body, 0)
    step(kn_ref[...], vn_ref[...])
    _diff_finish(lam_ref[0], lam_init, g_ref, o_ref, l0, a0, l1, a1)


def diff_attn_sample(q, k_new, v_new, k_cache, v_cache, lam, g, n_seq, seq, heads, lam_init):
    n, d = q.shape
    hw = d // heads
    hd = hw // 2
    past = k_cache.shape[1]
    tk = _tile(past, TK_DIFF, SUBLANES)
    stat = pltpu.VMEM((seq, 1), F32)
    acc = pltpu.VMEM((seq, hw), F32)
    blk = pl.BlockSpec((seq, hw), lambda b, h: (b, h))
    cache = pl.BlockSpec((1, past, hw), lambda b, h: (b, 0, h))
    return pl.pallas_call(
        functools.partial(_diff_sample_kernel, tk=tk, nkb=past // tk, hd=hd, scale=hd ** -0.5,
                          lam_init=lam_init),
        grid=(n_seq, heads),
        in_specs=[pl.BlockSpec(memory_space=pltpu.SMEM), blk, cache, cache, blk, blk,
                  pl.BlockSpec((1, hw), lambda b, h: (0, 0))],
        out_specs=blk,
        out_shape=jax.ShapeDtypeStruct((n, d), BF16),
        scratch_shapes=[stat, stat, acc, stat, stat, acc],
        compiler_params=_cp("parallel", "parallel"),
        name="diff_attn_sample",
    )(lam, q, k_cache, v_cache, k_new, v_new, g.reshape(1, hw))


def _mem_attn_kernel(q_ref, k_ref, v_ref, o_ref, *, heads, hd, scale):
    for h in range(heads):
        sl = slice(h * hd, (h + 1) * hd)
        s = _dot_nt(q_ref[:, sl], k_ref[:, sl].astype(BF16)) * scale
        p = jnp.exp(s - jnp.max(s, axis=-1, keepdims=True))
        l = jnp.sum(p, axis=-1, keepdims=True)
        o = _dot(p.astype(BF16), v_ref[:, sl].astype(BF16)) / l
        o_ref[:, sl] = o.astype(o_ref.dtype)


def mem_attn(q, mk, mv, n_seq, seq, heads):
    n, d = q.shape
    m = mk.shape[0] // n_seq
    hd = d // heads
    tq = _tile(seq, TQ_MEM, 16)
    qb = seq // tq
    kv = pl.BlockSpec((m, d), lambda b, i: (b, 0))
    return pl.pallas_call(
        functools.partial(_mem_attn_kernel, heads=heads, hd=hd, scale=hd ** -0.5),
        grid=(n_seq, qb),
        in_specs=[pl.BlockSpec((tq, d), lambda b, i: (b * qb + i, 0)), kv, kv],
        out_specs=pl.BlockSpec((tq, d), lambda b, i: (b * qb + i, 0)),
        out_shape=jax.ShapeDtypeStruct((n, d), BF16),
        compiler_params=_cp("parallel", "parallel"),
        name="mem_attn",
    )(q, mk, mv)


def _top_rows(v, k):
    rows = []
    for _ in range(k):
        m = jnp.max(v, axis=0, keepdims=True)
        rows.append(m)
        v = jnp.where(v == m, -jnp.inf, v)
    return rows


def _peer_prep_kernel(xt_ref, wq_ref, keys_ref, s_ref, e_ref, tau_ref, *, heads, nk):
    qt = _dot(wq_ref[...], xt_ref[...]).astype(BF16)
    half = keys_ref.shape[2]
    for h in range(heads):
        sc = [_dot(keys_ref[2 * h + j], qt[(2 * h + j) * half:(2 * h + j + 1) * half]) for j in (0, 1)]
        a = _top_rows(sc[0], PEER_TOPK)
        b = jnp.concatenate(_top_rows(sc[1], PEER_TOPK), axis=0)
        cand = jnp.concatenate([ar + b for ar in a], axis=0)
        tau = _top_rows(cand, PEER_TOPK)[-1]
        mx = a[0] + b[0:1]
        z = jnp.sum(jnp.where(cand >= tau, jnp.exp(cand - mx), 0.0), axis=0, keepdims=True)
        s_ref[2 * h] = sc[0]
        s_ref[2 * h + 1] = sc[1]
        e_ref[2 * h] = jnp.exp(sc[0] - a[0])
        e_ref[2 * h + 1] = jnp.exp(sc[1] - b[0:1]) / z
        tau_ref[h:h + 1, :] = tau


def peer_prep(xt, wq_t, keys):
    d, n = xt.shape
    hj, nk, half = keys.shape
    heads = hj // 2
    t = _tile(n, T_PEER, LANES)
    big = pl.BlockSpec((hj, nk, t), lambda i: (0, 0, i))
    return pl.pallas_call(
        functools.partial(_peer_prep_kernel, heads=heads, nk=nk),
        grid=(n // t,),
        in_specs=[pl.BlockSpec((d, t), lambda i: (0, i)),
                  pl.BlockSpec(wq_t.shape, lambda i: (0, 0)),
                  pl.BlockSpec(keys.shape, lambda i: (0, 0, 0))],
        out_specs=[big, big, pl.BlockSpec((heads, t), lambda i: (0, i))],
        out_shape=[jax.ShapeDtypeStruct((hj, nk, n), F32), jax.ShapeDtypeStruct((hj, nk, n), F32),
                   jax.ShapeDtypeStruct((heads, n), F32)],
        compiler_params=_cp("parallel"),
        name="peer_prep",
    )(xt, wq_t, keys)


def _gelu_tanh(x):
    return 0.5 * x * (1.0 + jnp.tanh(math.sqrt(2.0 / math.pi) * (x + 0.044715 * (x * x * x))))


def _peer_main_kernel(xt_ref, u_ref, vt_ref, s_ref, e_ref, tau_ref, o_ref, acc_ref, act_ref, h_ref,
                      *, heads, nk, ne, t):
    e = pl.program_id(1)

    @pl.when(e == 0)
    def _():
        acc_ref[...] = jnp.zeros_like(acc_ref)

    act_ref[...] = _dot(u_ref[...], xt_ref[...])
    groups = act_ref.shape[0] // nk

    def body(ii, carry):
        i = e * groups + ii
        r0 = pl.multiple_of(ii * nk, nk)
        i8 = pl.multiple_of((i // SUBLANES) * SUBLANES, SUBLANES)
        pick = lax.broadcasted_iota(jnp.int32, (SUBLANES, LANES), 0) == i % SUBLANES

        def first_key_row(ref, h, ls):
            return jnp.sum(jnp.where(pick, ref[2 * h, pl.ds(i8, SUBLANES), ls], 0.0), axis=0, keepdims=True)

        for tl in range(t // LANES):
            ls = slice(tl * LANES, (tl + 1) * LANES)
            gate = jnp.zeros((nk, LANES), F32)
            for h in range(heads):
                s0 = first_key_row(s_ref, h, ls)
                e0 = first_key_row(e_ref, h, ls)
                hit = s0 + s_ref[2 * h + 1, :, ls] >= tau_ref[h:h + 1, ls]
                gate = gate + jnp.where(hit, e0 * e_ref[2 * h + 1, :, ls], 0.0)
            h_ref[pl.ds(r0, nk), ls] = (gate * _gelu_tanh(act_ref[pl.ds(r0, nk), ls])).astype(BF16)
        return carry

    lax.fori_loop(0, groups, body, 0)
    acc_ref[...] += _dot(vt_ref[...], h_ref[...])

    @pl.when(e == ne - 1)
    def _():
        o_ref[...] = acc_ref[...].T


def peer_main(xt, u, vt, s, ex, tau):
    d, n = xt.shape
    hj, nk, _ = s.shape
    n_exp = u.shape[0]
    t = _tile(n, T_PEER, LANES)
    ec = _tile(n_exp, EC_PEER, nk)
    ne = n_exp // ec
    big = pl.BlockSpec((hj, nk, t), lambda i, e: (0, 0, i))
    return pl.pallas_call(
        functools.partial(_peer_main_kernel, heads=hj // 2, nk=nk, ne=ne, t=t),
        grid=(n // t, ne),
        in_specs=[pl.BlockSpec((d, t), lambda i, e: (0, i)),
                  pl.BlockSpec((ec, d), lambda i, e: (e, 0)),
                  pl.BlockSpec((d, ec), lambda i, e: (0, e)),
                  big, big,
                  pl.BlockSpec((hj // 2, t), lambda i, e: (0, i))],
        out_specs=pl.BlockSpec((t, d), lambda i, e: (i, 0)),
        out_shape=jax.ShapeDtypeStruct((n, d), F32),
        scratch_shapes=[pltpu.VMEM((d, t), F32), pltpu.VMEM((ec, t), F32), pltpu.VMEM((ec, t), BF16)],
        compiler_params=_cp("parallel", "arbitrary"),
        name="peer_main",
    )(xt, u, vt, s, ex, tau)


def _diff_lambda(lam_vecs, layer_idx):
    lam_init = 0.8 - 0.6 * math.exp(-0.3 * layer_idx)
    lv = lam_vecs.astype(F32)
    lam = jnp.exp(jnp.sum(lv[0] * lv[1])) - jnp.exp(jnp.sum(lv[2] * lv[3])) + lam_init
    return lam.reshape(1), lam_init


def kernel(x_prompt, x_sample, state_ssd_conv, state_ssd_h, cache_diff_k, cache_diff_v, cache_mem_k, cache_mem_v, mem_prompt, ln_g, ln_b, ssd_w_in, ssd_conv_w, ssd_conv_b, ssd_a_log, ssd_dt_bias, ssd_d, ssd_norm_g, ssd_w_out, diff_w_qkv, diff_lam, diff_subln_g, diff_w_o, mem_w_q, mem_w_kv, mem_w_o, peer_w_q, peer_keys, peer_u, peer_v):
    bp, lp, d = x_prompt.shape
    bs, ls, _ = x_sample.shape
    depth = ln_g.shape[0]
    alpha = (2.0 * depth) ** 0.25
    n_mixers = 2
    ssd_heads = ssd_a_log.shape[1]
    d_inner = ssd_norm_g.shape[1]
    conv_dim = ssd_conv_w.shape[2]
    d_state = state_ssd_h.shape[-1]
    groups = (conv_dim - d_inner) // (2 * d_state)
    hpg = ssd_heads // groups
    hd_ssd = d_inner // ssd_heads
    diff_heads = cache_diff_k.shape[3]
    mem_tokens, mem_heads = cache_mem_k.shape[2], cache_mem_k.shape[3]
    peer_heads, _, n_keys, peer_half = peer_keys.shape[1:]

    streams = [
        dict(x=x_prompt.reshape(bp * lp, d), n_seq=bp, seq=lp),
        dict(x=x_sample.reshape(bs * ls, d), n_seq=bs, seq=ls),
    ]
    outs = [dict(conv=[], h=[], dk=[], dv=[]), dict(conv=[], h=[], dk=[], dv=[])]
    mk_p, mv_p = [], []

    def group_lanes(v):
        v = v.reshape(v.shape[:-1] + (groups, hpg))
        v = jnp.pad(v, [(0, 0)] * (v.ndim - 1) + [(0, LANES - hpg)])
        return v.reshape(v.shape[:-2] + (groups * LANES,))

    for i in range(depth):
        j = i // n_mixers
        g0, b0 = ln_g[i, 0], ln_b[i, 0]
        if i % n_mixers == 0:
            w_in = ssd_w_in[j]
            w_z = w_in[:, :d_inner].astype(BF16)
            w_xbc = w_in[:, d_inner:d_inner + conv_dim].astype(BF16)
            w_dt = group_lanes(w_in[:, d_inner + conv_dim:]).astype(BF16)
            dtb = group_lanes(ssd_dt_bias[j]).reshape(1, -1)
            a_neg = group_lanes(-jnp.exp(ssd_a_log[j].astype(F32))).reshape(1, -1)
            d_e = jnp.repeat(ssd_d[j], hd_ssd).reshape(1, d_inner)
            ng = ssd_norm_g[j].reshape(1, d_inner)
            w_out = ssd_w_out[j].astype(BF16)
            for si, st in enumerate(streams):
                x, n_seq, seq = st["x"], st["n_seq"], st["seq"]
                z = matmul(x, w_z)
                xbc = matmul(x, w_xbc)
                dt_raw = matmul(x, w_dt)
                if si == 0:
                    state8 = jnp.zeros((n_seq, SUBLANES, conv_dim), F32)
                    h0 = jnp.zeros((n_seq, ssd_heads, hd_ssd, d_state), F32)
                else:
                    state8 = jnp.pad(state_ssd_conv[j], ((0, 0), (SUBLANES - (CONV_W - 1), 0), (0, 0)))
                    h0 = state_ssd_h[j]
                outs[si]["conv"].append(xbc.reshape(n_seq, seq, conv_dim)[:, seq - (CONV_W - 1):])
                xc = conv_silu(xbc, state8, ssd_conv_w[j], ssd_conv_b[j], n_seq, seq)
                y, h_new = ssd_scan(xc, z, dt_raw, dtb, a_neg, d_e, ng, h0, n_seq, seq,
                                    d_inner, d_state, groups)
                outs[si]["h"].append(h_new)
                st["x"] = matmul_ln(y, w_out, x, g0, b0, alpha)
        else:
            lam, lam_init = _diff_lambda(diff_lam[j], i)
            w_qkv = diff_w_qkv[j]
            w_q, w_k, w_v = (w_qkv[:, c * d:(c + 1) * d].astype(BF16) for c in range(3))
            w_o = diff_w_o[j].astype(BF16)
            for si, st in enumerate(streams):
                x, n_seq, seq = st["x"], st["n_seq"], st["seq"]
                q = matmul(x, w_q, (BF16,))
                k32, k16 = matmul(x, w_k, (F32, BF16))
                v32, v16 = matmul(x, w_v, (F32, BF16))
                hw = d // diff_heads
                outs[si]["dk"].append(k32.reshape(n_seq, seq, diff_heads, 2, hw // 2))
                outs[si]["dv"].append(v32.reshape(n_seq, seq, diff_heads, hw))
                if si == 0:
                    o = diff_attn_prompt(q, k16, v16, lam, diff_subln_g[j], n_seq, seq, diff_heads, lam_init)
                else:
                    past = cache_diff_k.shape[2]
                    o = diff_attn_sample(q, k16, v16, cache_diff_k[j].reshape(n_seq, past, d),
                                         cache_diff_v[j].reshape(n_seq, past, d), lam, diff_subln_g[j],
                                         n_seq, seq, diff_heads, lam_init)
                st["x"] = matmul_ln(o, w_o, x, g0, b0, alpha)

        w_mq = mem_w_q[i].astype(BF16)
        w_mk = mem_w_kv[i][:, :d].astype(BF16)
        w_mv = mem_w_kv[i][:, d:].astype(BF16)
        w_mo = mem_w_o[i].astype(BF16)
        mem_rows = mem_prompt.reshape(bp * mem_tokens, d)
        mk = matmul(mem_rows, w_mk)
        mv = matmul(mem_rows, w_mv)
        mk_p.append(mk.reshape(bp, mem_tokens, mem_heads, d // mem_heads))
        mv_p.append(mv.reshape(bp, mem_tokens, mem_heads, d // mem_heads))
        wq_t = peer_w_q[i].T.astype(BF16)
        keys = peer_keys[i].reshape(peer_heads * 2, n_keys, peer_half).astype(BF16)
        u = peer_u[i].astype(BF16)
        vt = peer_v[i].T.astype(BF16)
        for si, st in enumerate(streams):
            x, n_seq, seq = st["x"], st["n_seq"], st["seq"]
            q = matmul(x, w_mq, (BF16,))
            if si == 0:
                kk, vv = mk, mv
            else:
                kk = cache_mem_k[i].reshape(n_seq * mem_tokens, d)
                vv = cache_mem_v[i].reshape(n_seq * mem_tokens, d)
            o = mem_attn(q, kk, vv, n_seq, seq, mem_heads)
            x, xt = matmul_ln(o, w_mo, x, ln_g[i, 1], ln_b[i, 1], alpha, with_t=True)
            s, ex, tau = peer_prep(xt, wq_t, keys)
            f = peer_main(xt, u, vt, s, ex, tau)
            st["x"] = post_norm(x, f, ln_g[i, 2], ln_b[i, 2], alpha)

    yp = streams[0]["x"].reshape(bp, lp, d)
    ys = streams[1]["x"].reshape(bs, ls, d)
    o_p, o_s = outs
    return (yp, ys, jnp.stack(o_p["conv"]), jnp.stack(o_p["h"]), jnp.stack(o_p["dk"]), jnp.stack(o_p["dv"]),
            jnp.stack(mk_p), jnp.stack(mv_p), jnp.stack(o_s["conv"]), jnp.stack(o_s["h"]),
            jnp.stack(o_s["dk"]), jnp.stack(o_s["dv"]))
```

```python
import functools
import math

import jax
import jax.numpy as jnp
from jax import lax
from jax.experimental import pallas as pl
from jax.experimental.pallas import tpu as pltpu

F32 = jnp.float32
BF16 = jnp.bfloat16

LANES = 128
SUBLANES = 8
VMEM_LIMIT_BYTES = 56 * 1024 * 1024

MASK_CHUNK = 64
CONV_W = 4
PEER_TOPK = 16
LN_EPS = 1e-5
RMS_EPS = 1e-5
SSD_CHUNK = 128

TM = 1024
TN = 512
TM_LN = 512
TK = 1024
TQ_DIFF = 1024
TK_DIFF = 512
TQ_MEM = 512
T_PEER = 512
EC_PEER = 512
PEER_J_ROWS = 32
TR_CONV = 512
TC_CONV = 512

_NT = (((1,), (1,)), ((), ()))


def _tile(n, target, mult):
    t = (min(target, n) // mult) * mult
    while t >= mult:
        if n % t == 0:
            return t
        t -= mult
    return n


def _cp(*sem):
    return pltpu.CompilerParams(dimension_semantics=sem, vmem_limit_bytes=VMEM_LIMIT_BYTES)


def _dot(a, b):
    return jnp.dot(a, b, preferred_element_type=F32)


def _dot_nt(a, b):
    return lax.dot_general(a, b, _NT, preferred_element_type=F32)


def _layer_norm(h, g, b):
    mu = jnp.mean(h, axis=-1, keepdims=True)
    d = h - mu
    var = jnp.mean(d * d, axis=-1, keepdims=True)
    return d * lax.rsqrt(var + LN_EPS) * g + b


def _mm_kernel(a_ref, w_ref, *o_refs, scale):
    acc = _dot(a_ref[...].astype(BF16), w_ref[...])
    if scale is not None:
        acc = acc * scale
    for o in o_refs:
        o[...] = acc.astype(o.dtype)


def matmul(a, w, out_dtypes=(F32,), scale=None):
    n, k = a.shape
    m = w.shape[1]
    tm, tn = _tile(n, TM, 16), _tile(m, TN, LANES)
    outs = pl.pallas_call(
        functools.partial(_mm_kernel, scale=scale),
        grid=(n // tm, m // tn),
        in_specs=[pl.BlockSpec((tm, k), lambda i, j: (i, 0)),
                  pl.BlockSpec((k, tn), lambda i, j: (0, j))],
        out_specs=[pl.BlockSpec((tm, tn), lambda i, j: (i, j)) for _ in out_dtypes],
        out_shape=[jax.ShapeDtypeStruct((n, m), d) for d in out_dtypes],
        compiler_params=_cp("parallel", "parallel"),
        name="matmul",
    )(a, w)
    return outs[0] if len(out_dtypes) == 1 else outs


def _mm_ln_kernel(a_ref, w_ref, x_ref, g_ref, b_ref, *refs, alpha, nk, with_t):
    o_ref, acc_ref = refs[0], refs[-1]
    k = pl.program_id(1)

    @pl.when(k == 0)
    def _():
        acc_ref[...] = jnp.zeros_like(acc_ref)

    acc_ref[...] += _dot(a_ref[...].astype(BF16), w_ref[...])

    @pl.when(k == nk - 1)
    def _():
        y = _layer_norm(alpha * x_ref[...] + acc_ref[...], g_ref[...], b_ref[...])
        o_ref[...] = y
        if with_t:
            refs[1][...] = y.T.astype(BF16)


def matmul_ln(a, w, x, g, b, alpha, with_t=False):
    n, kdim = a.shape
    d = w.shape[1]
    tm, tk = _tile(n, TM_LN, LANES if with_t else 16), _tile(kdim, TK, LANES)
    nk = kdim // tk
    out_shape = [jax.ShapeDtypeStruct((n, d), F32)]
    out_specs = [pl.BlockSpec((tm, d), lambda i, k: (i, 0))]
    if with_t:
        out_shape.append(jax.ShapeDtypeStruct((d, n), BF16))
        out_specs.append(pl.BlockSpec((d, tm), lambda i, k: (0, i)))
    outs = pl.pallas_call(
        functools.partial(_mm_ln_kernel, alpha=alpha, nk=nk, with_t=with_t),
        grid=(n // tm, nk),
        in_specs=[pl.BlockSpec((tm, tk), lambda i, k: (i, k)),
                  pl.BlockSpec((tk, d), lambda i, k: (k, 0)),
                  pl.BlockSpec((tm, d), lambda i, k: (i, 0)),
                  pl.BlockSpec((1, d), lambda i, k: (0, 0)),
                  pl.BlockSpec((1, d), lambda i, k: (0, 0))],
        out_specs=out_specs,
        out_shape=out_shape,
        scratch_shapes=[pltpu.VMEM((tm, d), F32)],
        compiler_params=_cp("parallel", "arbitrary"),
        name="matmul_ln",
    )(a, w, x, g.reshape(1, d), b.reshape(1, d))
    return outs if with_t else outs[0]


def _post_norm_kernel(x_ref, f_ref, g_ref, b_ref, o_ref, *, alpha):
    o_ref[...] = _layer_norm(alpha * x_ref[...] + f_ref[...], g_ref[...], b_ref[...])


def post_norm(x, f, g, b, alpha):
    n, d = x.shape
    tm = _tile(n, TM_LN, SUBLANES)
    row = pl.BlockSpec((tm, d), lambda i: (i, 0))
    vec = pl.BlockSpec((1, d), lambda i: (0, 0))
    return pl.pallas_call(
        functools.partial(_post_norm_kernel, alpha=alpha),
        grid=(n // tm,),
        in_specs=[row, row, vec, vec],
        out_specs=row,
        out_shape=jax.ShapeDtypeStruct((n, d), F32),
        compiler_params=_cp("parallel"),
        name="post_norm",
    )(x, f, g.reshape(1, d), b.reshape(1, d))


def _conv_kernel(x_ref, halo_ref, st_ref, w_ref, b_ref, o_ref, *, tr):
    r = pl.program_id(1)
    x = x_ref[...]
    prev = jnp.where(r == 0, st_ref[0], halo_ref[...])
    ext = jnp.concatenate([prev, x], axis=0)
    w = w_ref[...]
    acc = b_ref[...] + w[0:1] * ext[5:5 + tr]
    acc = acc + w[1:2] * ext[6:6 + tr]
    acc = acc + w[2:3] * ext[7:7 + tr]
    acc = acc + w[3:4] * x
    o_ref[...] = acc * jax.nn.sigmoid(acc)


def conv_silu(xbc, state8, w, bias, n_seq, seq):
    n, c = xbc.shape
    tr, tc = _tile(seq, TR_CONV, SUBLANES), _tile(c, TC_CONV, LANES)
    rb = seq // tr

    def halo_map(s, r, j):
        return (jnp.maximum((s * seq + r * tr) // SUBLANES - 1, 0), j)

    return pl.pallas_call(
        functools.partial(_conv_kernel, tr=tr),
        grid=(n_seq, rb, c // tc),
        in_specs=[pl.BlockSpec((tr, tc), lambda s, r, j: (s * rb + r, j)),
                  pl.BlockSpec((SUBLANES, tc), halo_map),
                  pl.BlockSpec((1, SUBLANES, tc), lambda s, r, j: (s, 0, j)),
                  pl.BlockSpec((CONV_W, tc), lambda s, r, j: (0, j)),
                  pl.BlockSpec((1, tc), lambda s, r, j: (0, j))],
        out_specs=pl.BlockSpec((tr, tc), lambda s, r, j: (s * rb + r, j)),
        out_shape=jax.ShapeDtypeStruct((n, c), F32),
        compiler_params=_cp("parallel", "parallel", "parallel"),
        name="conv_silu",
    )(xbc, xbc, state8, w, bias.reshape(1, c))


def _ssd_kernel(x_ref, b_ref, c_ref, z_ref, dt_ref, dtb_ref, a_ref, d_ref, ng_ref, h0_ref,
                y_ref, hout_ref, ht_ref, *, rows, valid, nc, hpg, hd):
    c = pl.program_id(2)
    gw = hpg * hd

    @pl.when(c == 0)
    def _():
        ht_ref[...] = h0_ref[0].reshape(gw, h0_ref.shape[-1]).T

    def pad(v):
        if valid == rows:
            return v
        return jnp.concatenate([v, jnp.zeros((rows - valid, v.shape[1]), v.dtype)], axis=0)

    x = pad(x_ref[...])
    bm = pad(b_ref[...])
    cm = pad(c_ref[...])
    dt = pad(jax.nn.softplus(dt_ref[...] + dtb_ref[...]))
    da = dt * a_ref[...]
    ri = lax.broadcasted_iota(jnp.int32, (rows, rows), 0)
    ci = lax.broadcasted_iota(jnp.int32, (rows, rows), 1)
    tri = ri >= ci
    cum = jnp.dot(tri.astype(F32), da, precision=lax.Precision.HIGHEST,
                  preferred_element_type=F32)
    cum_t = cum.T
    ecum = jnp.exp(cum)
    tot = cum[rows - 1:rows, :]
    toend = jnp.exp(tot - cum)
    etot = jnp.exp(tot)

    cb16 = cm.astype(BF16)
    cb = _dot_nt(cb16, bm.astype(BF16))
    ht = ht_ref[...]
    yoff = _dot(cb16, ht.astype(BF16))
    ys, xws, ets = [], [], []
    for h in range(hpg):
        sl = slice(h * hd, (h + 1) * hd)
        xdt = x[:, sl] * dt[:, h:h + 1]
        dec = jnp.exp(jnp.where(tri, cum[:, h:h + 1] - cum_t[h:h + 1, :], -jnp.inf))
        yd = _dot((cb * dec).astype(BF16), xdt.astype(BF16))
        ys.append(yd + yoff[:, sl] * ecum[:, h:h + 1])
        xws.append(xdt * toend[:, h:h + 1])
        ets.append(jnp.broadcast_to(etot[:, h:h + 1], (1, hd)))
    y = jnp.concatenate(ys, axis=1)
    xw = jnp.concatenate(xws, axis=1)
    ht_new = ht * jnp.concatenate(ets, axis=1) + _dot(bm.T.astype(BF16), xw.astype(BF16))
    ht_ref[...] = ht_new

    z = pad(z_ref[...])
    y = (y + x * d_ref[...]) * (z * jax.nn.sigmoid(z))
    ms = jnp.mean(y * y, axis=-1, keepdims=True)
    yn = y * lax.rsqrt(ms + RMS_EPS) * ng_ref[...]
    y_ref[...] = yn[:valid].astype(y_ref.dtype)

    @pl.when(c == nc - 1)
    def _():
        hout_ref[0] = ht_new.T.reshape(hout_ref.shape[1:])


def ssd_scan(xbc, z, dt_raw, dtb, a, d_e, ng, h0, h0_off, n_seq, seq, d_inner, d_state, groups):
    n = xbc.shape[0]
    heads, hd = h0.shape[1], h0.shape[2]
    hpg = heads // groups
    gw = d_inner // groups
    assert gw == hpg * hd and d_state == LANES and gw % LANES == 0
    rows = SSD_CHUNK
    valid = rows if seq % rows == 0 else seq
    assert valid == rows or seq < rows
    nc = seq // valid
    boff, coff = d_inner // d_state, d_inner // d_state + groups
    row = lambda s, g, c: s * nc + c
    y, h_new = pl.pallas_call(
        functools.partial(_ssd_kernel, rows=rows, valid=valid, nc=nc, hpg=hpg, hd=hd),
        grid=(n_seq, groups, nc),
        in_specs=[pl.BlockSpec((valid, gw), lambda s, g, c: (row(s, g, c), g)),
                  pl.BlockSpec((valid, d_state), lambda s, g, c: (row(s, g, c), boff + g)),
                  pl.BlockSpec((valid, d_state), lambda s, g, c: (row(s, g, c), coff + g)),
                  pl.BlockSpec((valid, gw), lambda s, g, c: (row(s, g, c), g)),
                  pl.BlockSpec((valid, LANES), lambda s, g, c: (row(s, g, c), g)),
                  pl.BlockSpec((1, LANES), lambda s, g, c: (0, g)),
                  pl.BlockSpec((1, LANES), lambda s, g, c: (0, g)),
                  pl.BlockSpec((1, gw), lambda s, g, c: (0, g)),
                  pl.BlockSpec((1, gw), lambda s, g, c: (0, g)),
                  pl.BlockSpec((1, hpg, hd, d_state), lambda s, g, c: (h0_off + s, g, 0, 0))],
        out_specs=[pl.BlockSpec((valid, gw), lambda s, g, c: (row(s, g, c), g)),
                   pl.BlockSpec((1, hpg, hd, d_state), lambda s, g, c: (s, g, 0, 0))],
        out_shape=[jax.ShapeDtypeStruct((n, d_inner), BF16),
                   jax.ShapeDtypeStruct((n_seq,) + h0.shape[1:], F32)],
        scratch_shapes=[pltpu.VMEM((d_state, gw), F32)],
        compiler_params=_cp("parallel", "parallel", "arbitrary"),
        name="ssd_scan",
    )(xbc, xbc, xbc, z, dt_raw, dtb, a, d_e, ng, h0)
    return y, h_new


def _lanes(v, w):
    return pltpu.repeat(v, w // LANES, axis=1) if w >= LANES else v[:, :w]


def _softmax_step(s, m_ref, l_ref, acc_ref, vb):
    m_old = m_ref[...]
    m_new = jnp.maximum(m_old, jnp.max(s, axis=-1, keepdims=True))
    p = jnp.exp2(s - _lanes(m_new, s.shape[1]))
    corr = jnp.exp2(m_old - m_new)
    l_ref[...] = corr * l_ref[...] + jnp.sum(p, axis=-1, keepdims=True)
    acc_ref[...] = _lanes(corr, acc_ref.shape[1]) * acc_ref[...] + _dot(p.astype(BF16), vb)
    m_ref[...] = m_new


def _diff_finish(lam, lam_init, g_ref, o_ref, l0, a0, l1, a1):
    w = a0.shape[1]
    o = a0[...] / _lanes(l0[...], w) - lam * (a1[...] / _lanes(l1[...], w))
    ms = jnp.mean(o * o, axis=-1, keepdims=True)
    o_ref[...] = (o * lax.rsqrt(ms + RMS_EPS) * g_ref[...] * (1.0 - lam_init)).astype(o_ref.dtype)


def _diff_init(m0, l0, a0, m1, l1, a1):
    for m, l, a in ((m0, l0, a0), (m1, l1, a1)):
        m[...] = jnp.full_like(m, -jnp.inf)
        l[...] = jnp.zeros_like(l)
        a[...] = jnp.zeros_like(a)


def _diff_prompt_kernel(lam_ref, q_ref, k_ref, v_ref, g_ref, o_ref, m0, l0, a0, m1, l1, a1,
                        *, tq, tk, hd, lam_init):
    qi = pl.program_id(2)
    _diff_init(m0, l0, a0, m1, l1, a1)
    q = q_ref[...]
    q0, q1 = q[:, :hd], q[:, hd:]
    per_q = tq // tk

    def block(ki, mask):
        off = pl.multiple_of(ki * tk, tk)
        kb = k_ref[pl.ds(off, tk), :]
        vb = v_ref[pl.ds(off, tk), :]
        for qj, kj, m, l, a in ((q0, kb[:, :hd], m0, l0, a0), (q1, kb[:, hd:], m1, l1, a1)):
            s = _dot_nt(qj, kj)
            if mask is not None:
                s = jnp.where(mask, s, -jnp.inf)
            _softmax_step(s, m, l, a, vb)

    def body(ki, carry):
        block(ki, None)
        return carry

    lax.fori_loop(0, qi * per_q, body, 0)
    rc = lax.broadcasted_iota(jnp.int32, (tq, tk), 0) // MASK_CHUNK
    cc = lax.broadcasted_iota(jnp.int32, (tq, tk), 1) // MASK_CHUNK
    for d in range(per_q):
        block(qi * per_q + d, cc + d * (tk // MASK_CHUNK) <= rc)
    _diff_finish(lam_ref[0], lam_init, g_ref, o_ref, l0, a0, l1, a1)


def diff_attn_prompt(q, k, v, lam, g, n_seq, seq, heads, lam_init):
    n, d = q.shape
    hw = d // heads
    hd = hw // 2
    tq = _tile(seq, TQ_DIFF, MASK_CHUNK)
    tk = _tile(tq, TK_DIFF, MASK_CHUNK)
    qb = seq // tq
    stat = pltpu.VMEM((tq, LANES), F32)
    acc = pltpu.VMEM((tq, hw), F32)
    return pl.pallas_call(
        functools.partial(_diff_prompt_kernel, tq=tq, tk=tk, hd=hd, lam_init=lam_init),
        grid=(n_seq, heads, qb),
        in_specs=[pl.BlockSpec(memory_space=pltpu.SMEM),
                  pl.BlockSpec((tq, hw), lambda b, h, i: (b * qb + i, h)),
                  pl.BlockSpec((seq, hw), lambda b, h, i: (b, h)),
                  pl.BlockSpec((seq, hw), lambda b, h, i: (b, h)),
                  pl.BlockSpec((1, hw), lambda b, h, i: (0, 0))],
        out_specs=pl.BlockSpec((tq, hw), lambda b, h, i: (b * qb + i, h)),
        out_shape=jax.ShapeDtypeStruct((n, d), BF16),
        scratch_shapes=[stat, stat, acc, stat, stat, acc],
        compiler_params=_cp("parallel", "parallel", "arbitrary"),
        name="diff_attn_prompt",
    )(lam, q, k, v, g.reshape(1, hw))


def _diff_sample_kernel(lam_ref, q_ref, kc_ref, vc_ref, kn_ref, vn_ref, g_ref, o_ref,
                        m0, l0, a0, m1, l1, a1, *, tk, nkb, hd, lam_init):
    _diff_init(m0, l0, a0, m1, l1, a1)
    q = q_ref[...]
    q0, q1 = q[:, :hd], q[:, hd:]

    def step(kb, vb):
        for qj, kj, m, l, a in ((q0, kb[:, :hd], m0, l0, a0), (q1, kb[:, hd:], m1, l1, a1)):
            _softmax_step(_dot_nt(qj, kj), m, l, a, vb)

    def body(ki, carry):
        off = pl.multiple_of(ki * tk, tk)
        step(kc_ref[0, pl.ds(off, tk), :].astype(BF16), vc_ref[0, pl.ds(off, tk), :].astype(BF16))
        return carry

    lax.fori_loop(0, nkb, body, 0)
    step(kn_ref[...], vn_ref[...])
    _diff_finish(lam_ref[0], lam_init, g_ref, o_ref, l0, a0, l1, a1)


def diff_attn_sample(q, k_new, v_new, k_cache, v_cache, cache_off, lam, g, n_seq, seq, heads, lam_init):
    n, d = q.shape
    hw = d // heads
    hd = hw // 2
    past = k_cache.shape[1]
    tk = _tile(past, TK_DIFF, SUBLANES)
    stat = pltpu.VMEM((seq, LANES), F32)
    acc = pltpu.VMEM((seq, hw), F32)
    blk = pl.BlockSpec((seq, hw), lambda b, h: (b, h))
    cache = pl.BlockSpec((1, past, hw), lambda b, h: (cache_off + b, 0, h))
    return pl.pallas_call(
        functools.partial(_diff_sample_kernel, tk=tk, nkb=past // tk, hd=hd, lam_init=lam_init),
        grid=(n_seq, heads),
        in_specs=[pl.BlockSpec(memory_space=pltpu.SMEM), blk, cache, cache, blk, blk,
                  pl.BlockSpec((1, hw), lambda b, h: (0, 0))],
        out_specs=blk,
        out_shape=jax.ShapeDtypeStruct((n, d), BF16),
        scratch_shapes=[stat, stat, acc, stat, stat, acc],
        compiler_params=_cp("parallel", "parallel"),
        name="diff_attn_sample",
    )(lam, q, k_cache, v_cache, k_new, v_new, g.reshape(1, hw))


def _mem_attn_kernel(q_ref, k_ref, v_ref, o_ref, *, heads, hd, scale):
    for h in range(heads):
        sl = slice(h * hd, (h + 1) * hd)
        s = _dot_nt(q_ref[:, sl], k_ref[:, sl].astype(BF16)) * scale
        p = jnp.exp(s - jnp.max(s, axis=-1, keepdims=True))
        l = jnp.sum(p, axis=-1, keepdims=True)
        o = _dot(p.astype(BF16), v_ref[:, sl].astype(BF16)) / l
        o_ref[:, sl] = o.astype(o_ref.dtype)


def mem_attn(q, mk, mv, kv_off, m, n_seq, seq, heads):
    n, d = q.shape
    hd = d // heads
    tq = _tile(seq, TQ_MEM, 16)
    qb = seq // tq
    kv = pl.BlockSpec((m, d), lambda b, i: (kv_off + b, 0))
    return pl.pallas_call(
        functools.partial(_mem_attn_kernel, heads=heads, hd=hd, scale=hd ** -0.5),
        grid=(n_seq, qb),
        in_specs=[pl.BlockSpec((tq, d), lambda b, i: (b * qb + i, 0)), kv, kv],
        out_specs=pl.BlockSpec((tq, d), lambda b, i: (b * qb + i, 0)),
        out_shape=jax.ShapeDtypeStruct((n, d), BF16),
        compiler_params=_cp("parallel", "parallel"),
        name="mem_attn",
    )(q, mk, mv)


def _top_rows(v, k):
    rows = []
    for _ in range(k):
        m = jnp.max(v, axis=0, keepdims=True)
        rows.append(m)
        v = jnp.where(v == m, -jnp.inf, v)
    return rows


def _peer_prep_kernel(xt_ref, wq_ref, keys_ref, s0_ref, e0_ref, s1_ref, e1_ref, tau_ref, *, heads, gpc):
    k = PEER_TOPK
    qt = _dot(wq_ref[...], xt_ref[...]).astype(BF16)
    half = keys_ref.shape[2]
    for h in range(heads):
        sc = [_dot(keys_ref[2 * h + j], qt[(2 * h + j) * half:(2 * h + j + 1) * half]) for j in (0, 1)]
        a = _top_rows(sc[0], k)
        b = jnp.concatenate(_top_rows(sc[1], k), axis=0)
        cand = [a[0] + b] + [a[r] + b[:k // 2] for r in range(1, k // 2)]
        cand.append(jnp.concatenate(a[k // 2:], axis=0) + b[0:1])
        cand = jnp.concatenate(cand, axis=0)
        tau = _top_rows(cand, k)[-1]
        mx = a[0] + b[0:1]
        z = jnp.sum(jnp.where(cand >= tau, jnp.exp(cand - mx), 0.0), axis=0, keepdims=True)
        e0 = jnp.exp(sc[0] - a[0])
        for c in range(s0_ref.shape[1]):
            s0_ref[h, c] = sc[0][c * gpc:(c + 1) * gpc]
            e0_ref[h, c] = e0[c * gpc:(c + 1) * gpc]
        s1_ref[h] = sc[1]
        e1_ref[h] = jnp.exp(sc[1] - b[0:1]) / z
        tau_ref[h:h + 1, :] = tau


def peer_prep(xt, wq_t, keys, gpc):
    d, n = xt.shape
    hj, nk, half = keys.shape
    heads = hj // 2
    assert PEER_TOPK % 2 == 0 and nk % gpc == 0
    t = _tile(n, T_PEER, LANES)
    first = pl.BlockSpec((heads, nk // gpc, gpc, t), lambda i: (0, 0, 0, i))
    second = pl.BlockSpec((heads, nk, t), lambda i: (0, 0, i))
    f_shape = jax.ShapeDtypeStruct((heads, nk // gpc, gpc, n), F32)
    s_shape = jax.ShapeDtypeStruct((heads, nk, n), F32)
    return pl.pallas_call(
        functools.partial(_peer_prep_kernel, heads=heads, gpc=gpc),
        grid=(n // t,),
        in_specs=[pl.BlockSpec((d, t), lambda i: (0, i)),
                  pl.BlockSpec(wq_t.shape, lambda i: (0, 0)),
                  pl.BlockSpec(keys.shape, lambda i: (0, 0, 0))],
        out_specs=[first, first, second, second, pl.BlockSpec((heads, t), lambda i: (0, i))],
        out_shape=[f_shape, f_shape, s_shape, s_shape, jax.ShapeDtypeStruct((heads, n), F32)],
        compiler_params=_cp("parallel"),
        name="peer_prep",
    )(xt, wq_t, keys)


def _gelu_tanh(x):
    return 0.5 * x * (1.0 + jnp.tanh(math.sqrt(2.0 / math.pi) * (x + 0.044715 * (x * x * x))))


def _peer_mix(s0_ref, e0_ref, s1_ref, e1_ref, tau_ref, act_ref, h_ref, *, heads, nk, t):
    gpc = act_ref.shape[0] // nk
    jh = PEER_J_ROWS
    for tl in range(t // LANES):
        ls = slice(tl * LANES, (tl + 1) * LANES)
        for jb in range(nk // jh):
            js = slice(jb * jh, (jb + 1) * jh)
            gates = [jnp.zeros((jh, LANES), F32) for _ in range(gpc)]
            for h in range(heads):
                s1, e1, tau = s1_ref[h, js, ls], e1_ref[h, js, ls], tau_ref[h:h + 1, ls]
                for ii in range(gpc):
                    hit = s0_ref[h, 0, ii:ii + 1, ls] + s1 >= tau
                    gates[ii] = gates[ii] + jnp.where(hit, e0_ref[h, 0, ii:ii + 1, ls] * e1, 0.0)
            for ii in range(gpc):
                rows = slice(ii * nk + jb * jh, ii * nk + (jb + 1) * jh)
                h_ref[rows, ls] = (gates[ii] * _gelu_tanh(act_ref[rows, ls])).astype(BF16)


def _peer_main_kernel(xt_ref, u_ref, vt_ref, s0a_ref, e0a_ref, s0b_ref, e0b_ref, s1_ref, e1_ref, tau_ref,
                      o_ref, acc_ref, act_a, act_b, h_a, h_b, *, heads, nk, ns, t):
    g = pl.program_id(1)
    ec = act_a.shape[0]
    mix = functools.partial(_peer_mix, s1_ref=s1_ref, e1_ref=e1_ref, tau_ref=tau_ref, heads=heads, nk=nk, t=t)

    @pl.when(g == 0)
    def _():
        acc_ref[...] = jnp.zeros_like(acc_ref)
        act_b[...] = jnp.zeros_like(act_b)
        h_a[...] = jnp.zeros_like(h_a)
        h_b[...] = jnp.zeros_like(h_b)

    xt = xt_ref[...]
    out_a = _dot(vt_ref[:, :ec], h_a[...])
    mix(s0b_ref, e0b_ref, act_ref=act_b, h_ref=h_b)
    act_a[...] = _dot(u_ref[:ec, :], xt)
    out_b = _dot(vt_ref[:, ec:], h_b[...])
    mix(s0a_ref, e0a_ref, act_ref=act_a, h_ref=h_a)
    act_b[...] = _dot(u_ref[ec:, :], xt)
    acc_ref[...] += out_a + out_b

    @pl.when(g == ns - 1)
    def _():
        o_ref[...] = acc_ref[...].T


def peer_main(xt, u, vt, s0, e0, s1, e1, tau):
    d, n = xt.shape
    heads, nchunk, gpc, _ = s0.shape
    nk = s1.shape[1]
    n_exp = u.shape[0]
    ec = gpc * nk
    assert n_exp == nchunk * ec and nchunk % 2 == 0
    t = _tile(n, T_PEER, LANES)
    ns = nchunk // 2 + 1
    last = nchunk // 2 - 1
    first_a = pl.BlockSpec((heads, 1, gpc, t), lambda i, g: (0, jnp.minimum(2 * g, nchunk - 1), 0, i))
    first_b = pl.BlockSpec((heads, 1, gpc, t), lambda i, g: (0, jnp.maximum(2 * g - 1, 0), 0, i))
    second = pl.BlockSpec((heads, nk, t), lambda i, g: (0, 0, i))
    return pl.pallas_call(
        functools.partial(_peer_main_kernel, heads=heads, nk=nk, ns=ns, t=t),
        grid=(n // t, ns),
        in_specs=[pl.BlockSpec((d, t), lambda i, g: (0, i)),
                  pl.BlockSpec((2 * ec, d), lambda i, g: (jnp.minimum(g, last), 0)),
                  pl.BlockSpec((d, 2 * ec), lambda i, g: (0, jnp.maximum(g - 1, 0))),
                  first_a, first_a, first_b, first_b, second, second,
                  pl.BlockSpec((heads, t), lambda i, g: (0, i))],
        out_specs=pl.BlockSpec((t, d), lambda i, g: (i, 0)),
        out_shape=jax.ShapeDtypeStruct((n, d), F32),
        scratch_shapes=[pltpu.VMEM((d, t), F32), pltpu.VMEM((ec, t), F32), pltpu.VMEM((ec, t), F32),
                        pltpu.VMEM((ec, t), BF16), pltpu.VMEM((ec, t), BF16)],
        compiler_params=_cp("parallel", "arbitrary"),
        name="peer_main",
    )(xt, u, vt, s0, e0, s0, e0, s1, e1, tau)


def _diff_lambda(lam_vecs, layer_idx):
    lam_init = 0.8 - 0.6 * math.exp(-0.3 * layer_idx)
    lv = lam_vecs.astype(F32)
    lam = jnp.exp(jnp.sum(lv[0] * lv[1])) - jnp.exp(jnp.sum(lv[2] * lv[3])) + lam_init
    return lam.reshape(1), lam_init


def kernel(x_prompt, x_sample, state_ssd_conv, state_ssd_h, cache_diff_k, cache_diff_v, cache_mem_k, cache_mem_v, mem_prompt, ln_g, ln_b, ssd_w_in, ssd_conv_w, ssd_conv_b, ssd_a_log, ssd_dt_bias, ssd_d, ssd_norm_g, ssd_w_out, diff_w_qkv, diff_lam, diff_subln_g, diff_w_o, mem_w_q, mem_w_kv, mem_w_o, peer_w_q, peer_keys, peer_u, peer_v):
    bp, lp, d = x_prompt.shape
    bs, ls, _ = x_sample.shape
    depth = ln_g.shape[0]
    alpha = (2.0 * depth) ** 0.25
    n_mixers = 2
    ssd_heads = ssd_a_log.shape[1]
    d_inner = ssd_norm_g.shape[1]
    conv_dim = ssd_conv_w.shape[2]
    d_state = state_ssd_h.shape[-1]
    groups = (conv_dim - d_inner) // (2 * d_state)
    hpg = ssd_heads // groups
    hd_ssd = d_inner // ssd_heads
    diff_heads = cache_diff_k.shape[3]
    mem_tokens, mem_heads = cache_mem_k.shape[2], cache_mem_k.shape[3]
    peer_heads, _, n_keys, peer_half = peer_keys.shape[1:]

    streams = [
        dict(x=x_prompt.reshape(bp * lp, d), n_seq=bp, seq=lp),
        dict(x=x_sample.reshape(bs * ls, d), n_seq=bs, seq=ls),
    ]
    outs = [dict(conv=[], h=[], dk=[], dv=[]), dict(conv=[], h=[], dk=[], dv=[])]
    mk_p, mv_p = [], []

    def group_lanes(v):
        v = v.reshape(v.shape[:-1] + (groups, hpg))
        v = jnp.pad(v, [(0, 0)] * (v.ndim - 1) + [(0, LANES - hpg)])
        return v.reshape(v.shape[:-2] + (groups * LANES,))

    for i in range(depth):
        j = i // n_mixers
        g0, b0 = ln_g[i, 0], ln_b[i, 0]
        if i % n_mixers == 0:
            w_in = ssd_w_in[j]
            w_z = w_in[:, :d_inner].astype(BF16)
            w_xbc = w_in[:, d_inner:d_inner + conv_dim].astype(BF16)
            w_dt = group_lanes(w_in[:, d_inner + conv_dim:]).astype(BF16)
            dtb = group_lanes(ssd_dt_bias[j]).reshape(1, -1)
            a_neg = group_lanes(-jnp.exp(ssd_a_log[j].astype(F32))).reshape(1, -1)
            d_e = jnp.repeat(ssd_d[j], hd_ssd).reshape(1, d_inner)
            ng = ssd_norm_g[j].reshape(1, d_inner)
            w_out = ssd_w_out[j].astype(BF16)
            for si, st in enumerate(streams):
                x, n_seq, seq = st["x"], st["n_seq"], st["seq"]
                z = matmul(x, w_z)
                xbc = matmul(x, w_xbc)
                dt_raw = matmul(x, w_dt)
                if si == 0:
                    state8 = jnp.zeros((n_seq, SUBLANES, conv_dim), F32)
                    h0, h0_off = jnp.zeros((n_seq, ssd_heads, hd_ssd, d_state), F32), 0
                else:
                    state8 = jnp.pad(state_ssd_conv[j], ((0, 0), (SUBLANES - (CONV_W - 1), 0), (0, 0)))
                    h0, h0_off = state_ssd_h.reshape((-1,) + state_ssd_h.shape[2:]), j * n_seq
                outs[si]["conv"].append(xbc.reshape(n_seq, seq, conv_dim)[:, seq - (CONV_W - 1):])
                xc = conv_silu(xbc, state8, ssd_conv_w[j], ssd_conv_b[j], n_seq, seq)
                y, h_new = ssd_scan(xc, z, dt_raw, dtb, a_neg, d_e, ng, h0, h0_off, n_seq, seq,
                                    d_inner, d_state, groups)
                outs[si]["h"].append(h_new)
                st["x"] = matmul_ln(y, w_out, x, g0, b0, alpha)
        else:
            lam, lam_init = _diff_lambda(diff_lam[j], i)
            w_qkv = diff_w_qkv[j]
            w_q, w_k, w_v = (w_qkv[:, c * d:(c + 1) * d].astype(BF16) for c in range(3))
            w_o = diff_w_o[j].astype(BF16)
            for si, st in enumerate(streams):
                x, n_seq, seq = st["x"], st["n_seq"], st["seq"]
                hd_diff = d // diff_heads // 2
                q = matmul(x, w_q, (BF16,), scale=hd_diff ** -0.5 * math.log2(math.e))
                k32, k16 = matmul(x, w_k, (F32, BF16))
                v32, v16 = matmul(x, w_v, (F32, BF16))
                hw = d // diff_heads
                outs[si]["dk"].append(k32.reshape(n_seq, seq, diff_heads, 2, hw // 2))
                outs[si]["dv"].append(v32.reshape(n_seq, seq, diff_heads, hw))
                if si == 0:
                    o = diff_attn_prompt(q, k16, v16, lam, diff_subln_g[j], n_seq, seq, diff_heads, lam_init)
                else:
                    past = cache_diff_k.shape[2]
                    o = diff_attn_sample(q, k16, v16, cache_diff_k.reshape(-1, past, d),
                                         cache_diff_v.reshape(-1, past, d), j * n_seq, lam, diff_subln_g[j],
                                         n_seq, seq, diff_heads, lam_init)
                st["x"] = matmul_ln(o, w_o, x, g0, b0, alpha)

        w_mq = mem_w_q[i].astype(BF16)
        w_mk = mem_w_kv[i][:, :d].astype(BF16)
        w_mv = mem_w_kv[i][:, d:].astype(BF16)
        w_mo = mem_w_o[i].astype(BF16)
        mem_rows = mem_prompt.reshape(bp * mem_tokens, d)
        mk = matmul(mem_rows, w_mk)
        mv = matmul(mem_rows, w_mv)
        mk_p.append(mk.reshape(bp, mem_tokens, mem_heads, d // mem_heads))
        mv_p.append(mv.reshape(bp, mem_tokens, mem_heads, d // mem_heads))
        wq_t = peer_w_q[i].T.astype(BF16)
        keys = peer_keys[i].reshape(peer_heads * 2, n_keys, peer_half).astype(BF16)
        u = peer_u[i].astype(BF16)
        vt = peer_v[i].T.astype(BF16)
        for si, st in enumerate(streams):
            x, n_seq, seq = st["x"], st["n_seq"], st["seq"]
            q = matmul(x, w_mq, (BF16,))
            if si == 0:
                kk, vv, kv_off = mk, mv, 0
            else:
                kk, vv, kv_off = cache_mem_k.reshape(-1, d), cache_mem_v.reshape(-1, d), i * n_seq
            o = mem_attn(q, kk, vv, kv_off, mem_tokens, n_seq, seq, mem_heads)
            x, xt = matmul_ln(o, w_mo, x, ln_g[i, 1], ln_b[i, 1], alpha, with_t=True)
            s0, e0, s1, e1, tau = peer_prep(xt, wq_t, keys, EC_PEER // n_keys)
            f = peer_main(xt, u, vt, s0, e0, s1, e1, tau)
            st["x"] = post_norm(x, f, ln_g[i, 2], ln_b[i, 2], alpha)

    yp = streams[0]["x"].reshape(bp, lp, d)
    ys = streams[1]["x"].reshape(bs, ls, d)
    o_p, o_s = outs
    return (yp, ys, jnp.stack(o_p["conv"]), jnp.stack(o_p["h"]), jnp.stack(o_p["dk"]), jnp.stack(o_p["dv"]),
            jnp.stack(mk_p), jnp.stack(mv_p), jnp.stack(o_s["conv"]), jnp.stack(o_s["h"]),
            jnp.stack(o_s["dk"]), jnp.stack(o_s["dv"]))
```

```python
import functools
import math

import jax
import jax.numpy as jnp
from jax import lax
from jax.experimental import pallas as pl
from jax.experimental.pallas import tpu as pltpu

F32 = jnp.float32
BF16 = jnp.bfloat16

LANES = 128
SUBLANES = 8
VMEM_LIMIT_BYTES = 56 * 1024 * 1024

MASK_CHUNK = 64
CONV_W = 4
PEER_TOPK = 16
LN_EPS = 1e-5
RMS_EPS = 1e-5
SSD_CHUNK = 128

TM = 1024
TN = 512
TM_LN = 512
TK = 2048
TQ_DIFF = 1024
TK_DIFF = 512
TQ_MEM = 512
T_PEER = 512
EC_PEER = 512
PEER_J_ROWS = 32
TR_CONV = 512
TC_CONV = 512

_NT = (((1,), (1,)), ((), ()))


def _tile(n, target, mult):
    t = (min(target, n) // mult) * mult
    while t >= mult:
        if n % t == 0:
            return t
        t -= mult
    return n


def _cp(*sem):
    return pltpu.CompilerParams(dimension_semantics=sem, vmem_limit_bytes=VMEM_LIMIT_BYTES)


def _dot(a, b):
    return jnp.dot(a, b, preferred_element_type=F32)


def _dot_nt(a, b):
    return lax.dot_general(a, b, _NT, preferred_element_type=F32)


def _layer_norm(h, g, b):
    mu = jnp.mean(h, axis=-1, keepdims=True)
    d = h - mu
    var = jnp.mean(d * d, axis=-1, keepdims=True)
    return d * lax.rsqrt(var + LN_EPS) * g + b


def _mm_kernel(a_ref, w_ref, *o_refs, scale):
    acc = _dot(a_ref[...].astype(BF16), w_ref[...])
    if scale is not None:
        acc = acc * scale
    for o in o_refs:
        o[...] = acc.astype(o.dtype)


def matmul(a, w, out_dtypes=(F32,), scale=None):
    n, k = a.shape
    m = w.shape[1]
    tm, tn = _tile(n, TM, 16), _tile(m, TN, LANES)
    outs = pl.pallas_call(
        functools.partial(_mm_kernel, scale=scale),
        grid=(n // tm, m // tn),
        in_specs=[pl.BlockSpec((tm, k), lambda i, j: (i, 0)),
                  pl.BlockSpec((k, tn), lambda i, j: (0, j))],
        out_specs=[pl.BlockSpec((tm, tn), lambda i, j: (i, j)) for _ in out_dtypes],
        out_shape=[jax.ShapeDtypeStruct((n, m), d) for d in out_dtypes],
        compiler_params=_cp("parallel", "parallel"),
        name="matmul",
    )(a, w)
    return outs[0] if len(out_dtypes) == 1 else outs


def _mm_rows_kernel(a_ref, w_ref, lin_ref, o16_ref, *, tm, order):
    acc = _dot(a_ref[...].astype(BF16), w_ref[...])
    o16_ref[...] = acc.astype(BF16)
    nchunk = len(order)
    for c, r in enumerate(order):
        lin_ref[pl.ds(r, tm, stride=nchunk), :] = acc[:, c * LANES:(c + 1) * LANES]


def matmul_rows(a, w, order):
    n, k = a.shape
    m = w.shape[1]
    nchunk = m // LANES
    tm = _tile(n, TM_LN, 16)
    lin, o16 = pl.pallas_call(
        functools.partial(_mm_rows_kernel, tm=tm, order=tuple(order)),
        grid=(n // tm,),
        in_specs=[pl.BlockSpec((tm, k), lambda i: (i, 0)),
                  pl.BlockSpec((k, m), lambda i: (0, 0))],
        out_specs=[pl.BlockSpec((tm * nchunk, LANES), lambda i: (i, 0)),
                   pl.BlockSpec((tm, m), lambda i: (i, 0))],
        out_shape=[jax.ShapeDtypeStruct((n * nchunk, LANES), F32), jax.ShapeDtypeStruct((n, m), BF16)],
        compiler_params=_cp("parallel"),
        name="matmul_rows",
    )(a, w)
    return lin, o16


def _mm_ln_kernel(a_ref, w_ref, x_ref, g_ref, b_ref, *refs, alpha, nk, with_t):
    o_ref = refs[0]
    acc_ref = refs[-1] if nk > 1 else None
    k = pl.program_id(1)
    part = _dot(a_ref[...].astype(BF16), w_ref[...])

    def finish(f):
        y = _layer_norm(alpha * x_ref[...] + f, g_ref[...], b_ref[...])
        o_ref[...] = y
        if with_t:
            refs[1][...] = y.T.astype(BF16)

    if nk == 1:
        finish(part)
        return

    @pl.when(k == 0)
    def _():
        acc_ref[...] = part

    @pl.when(jnp.logical_and(k > 0, k < nk - 1))
    def _():
        acc_ref[...] += part

    @pl.when(k == nk - 1)
    def _():
        finish(acc_ref[...] + part)


def matmul_ln(a, w, x, g, b, alpha, with_t=False):
    n, kdim = a.shape
    d = w.shape[1]
    tm, tk = _tile(n, TM_LN, LANES if with_t else 16), _tile(kdim, TK, LANES)
    nk = kdim // tk
    out_shape = [jax.ShapeDtypeStruct((n, d), F32)]
    out_specs = [pl.BlockSpec((tm, d), lambda i, k: (i, 0))]
    if with_t:
        out_shape.append(jax.ShapeDtypeStruct((d, n), BF16))
        out_specs.append(pl.BlockSpec((d, tm), lambda i, k: (0, i)))
    outs = pl.pallas_call(
        functools.partial(_mm_ln_kernel, alpha=alpha, nk=nk, with_t=with_t),
        grid=(n // tm, nk),
        in_specs=[pl.BlockSpec((tm, tk), lambda i, k: (i, k)),
                  pl.BlockSpec((tk, d), lambda i, k: (k, 0)),
                  pl.BlockSpec((tm, d), lambda i, k: (i, 0)),
                  pl.BlockSpec((1, d), lambda i, k: (0, 0)),
                  pl.BlockSpec((1, d), lambda i, k: (0, 0))],
        out_specs=out_specs,
        out_shape=out_shape,
        scratch_shapes=[pltpu.VMEM((tm, d), F32)] if nk > 1 else [],
        compiler_params=_cp("parallel", "arbitrary"),
        name="matmul_ln",
    )(a, w, x, g.reshape(1, d), b.reshape(1, d))
    return outs if with_t else outs[0]


def _conv_kernel(x_ref, halo_ref, st_ref, w_ref, b_ref, o_ref, *, tr):
    r = pl.program_id(1)
    x = x_ref[...]
    prev = jnp.where(r == 0, st_ref[0], halo_ref[...])
    ext = jnp.concatenate([prev, x], axis=0)
    w = w_ref[...]
    acc = b_ref[...] + w[0:1] * ext[5:5 + tr]
    acc = acc + w[1:2] * ext[6:6 + tr]
    acc = acc + w[2:3] * ext[7:7 + tr]
    acc = acc + w[3:4] * x
    o_ref[...] = acc * jax.nn.sigmoid(acc)


def conv_silu(xbc, state8, w, bias, n_seq, seq):
    n, c = xbc.shape
    tr, tc = _tile(seq, TR_CONV, SUBLANES), _tile(c, TC_CONV, LANES)
    rb = seq // tr

    def halo_map(s, r, j):
        return (jnp.maximum((s * seq + r * tr) // SUBLANES - 1, 0), j)

    return pl.pallas_call(
        functools.partial(_conv_kernel, tr=tr),
        grid=(n_seq, rb, c // tc),
        in_specs=[pl.BlockSpec((tr, tc), lambda s, r, j: (s * rb + r, j)),
                  pl.BlockSpec((SUBLANES, tc), halo_map),
                  pl.BlockSpec((1, SUBLANES, tc), lambda s, r, j: (s, 0, j)),
                  pl.BlockSpec((CONV_W, tc), lambda s, r, j: (0, j)),
                  pl.BlockSpec((1, tc), lambda s, r, j: (0, j))],
        out_specs=pl.BlockSpec((tr, tc), lambda s, r, j: (s * rb + r, j)),
        out_shape=jax.ShapeDtypeStruct((n, c), F32),
        compiler_params=_cp("parallel", "parallel", "parallel"),
        name="conv_silu",
    )(xbc, xbc, state8, w, bias.reshape(1, c))


def _ssd_kernel(x_ref, b_ref, c_ref, z_ref, dt_ref, dtb_ref, a_ref, d_ref, ng_ref, h0_ref,
                y_ref, hout_ref, ht_ref, *, rows, valid, nc, hpg, hd):
    c = pl.program_id(2)
    gw = hpg * hd

    @pl.when(c == 0)
    def _():
        ht_ref[...] = h0_ref[0].reshape(gw, h0_ref.shape[-1]).T

    def pad(v):
        if valid == rows:
            return v
        return jnp.concatenate([v, jnp.zeros((rows - valid, v.shape[1]), v.dtype)], axis=0)

    x = pad(x_ref[...])
    bm = pad(b_ref[...])
    cm = pad(c_ref[...])
    dt = pad(jax.nn.softplus(dt_ref[...] + dtb_ref[...]))
    da = dt * a_ref[...]
    ri = lax.broadcasted_iota(jnp.int32, (rows, rows), 0)
    ci = lax.broadcasted_iota(jnp.int32, (rows, rows), 1)
    tri = ri >= ci
    cum = jnp.dot(tri.astype(F32), da, precision=lax.Precision.HIGHEST,
                  preferred_element_type=F32)
    cum_t = cum.T
    ecum = jnp.exp(cum)
    tot = cum[rows - 1:rows, :]
    toend = jnp.exp(tot - cum)
    etot = jnp.exp(tot)

    cb16 = cm.astype(BF16)
    cb = _dot_nt(cb16, bm.astype(BF16))
    ht = ht_ref[...]
    yoff = _dot(cb16, ht.astype(BF16))
    ys, xws, ets = [], [], []
    for h in range(hpg):
        sl = slice(h * hd, (h + 1) * hd)
        xdt = x[:, sl] * dt[:, h:h + 1]
        dec = jnp.exp(jnp.where(tri, cum[:, h:h + 1] - cum_t[h:h + 1, :], -jnp.inf))
        yd = _dot((cb * dec).astype(BF16), xdt.astype(BF16))
        ys.append(yd + yoff[:, sl] * ecum[:, h:h + 1])
        xws.append(xdt * toend[:, h:h + 1])
        ets.append(jnp.broadcast_to(etot[:, h:h + 1], (1, hd)))
    y = jnp.concatenate(ys, axis=1)
    xw = jnp.concatenate(xws, axis=1)
    ht_new = ht * jnp.concatenate(ets, axis=1) + _dot(bm.T.astype(BF16), xw.astype(BF16))
    ht_ref[...] = ht_new

    z = pad(z_ref[...])
    y = (y + x * d_ref[...]) * (z * jax.nn.sigmoid(z))
    ms = jnp.mean(y * y, axis=-1, keepdims=True)
    yn = y * lax.rsqrt(ms + RMS_EPS) * ng_ref[...]
    y_ref[...] = yn[:valid].astype(y_ref.dtype)

    @pl.when(c == nc - 1)
    def _():
        hout_ref[0] = ht_new.T.reshape(hout_ref.shape[1:])


def ssd_scan(xbc, z, dt_raw, dtb, a, d_e, ng, h0, h0_off, n_seq, seq, d_inner, d_state, groups):
    n = xbc.shape[0]
    heads, hd = h0.shape[1], h0.shape[2]
    hpg = heads // groups
    gw = d_inner // groups
    assert gw == hpg * hd and d_state == LANES and gw % LANES == 0
    rows = SSD_CHUNK
    valid = rows if seq % rows == 0 else seq
    assert valid == rows or seq < rows
    nc = seq // valid
    boff, coff = d_inner // d_state, d_inner // d_state + groups
    row = lambda s, g, c: s * nc + c
    y, h_new = pl.pallas_call(
        functools.partial(_ssd_kernel, rows=rows, valid=valid, nc=nc, hpg=hpg, hd=hd),
        grid=(n_seq, groups, nc),
        in_specs=[pl.BlockSpec((valid, gw), lambda s, g, c: (row(s, g, c), g)),
                  pl.BlockSpec((valid, d_state), lambda s, g, c: (row(s, g, c), boff + g)),
                  pl.BlockSpec((valid, d_state), lambda s, g, c: (row(s, g, c), coff + g)),
                  pl.BlockSpec((valid, gw), lambda s, g, c: (row(s, g, c), g)),
                  pl.BlockSpec((valid, LANES), lambda s, g, c: (row(s, g, c), g)),
                  pl.BlockSpec((1, LANES), lambda s, g, c: (0, g)),
                  pl.BlockSpec((1, LANES), lambda s, g, c: (0, g)),
                  pl.BlockSpec((1, gw), lambda s, g, c: (0, g)),
                  pl.BlockSpec((1, gw), lambda s, g, c: (0, g)),
                  pl.BlockSpec((1, hpg, hd, d_state), lambda s, g, c: (h0_off + s, g, 0, 0))],
        out_specs=[pl.BlockSpec((valid, gw), lambda s, g, c: (row(s, g, c), g)),
                   pl.BlockSpec((1, hpg, hd, d_state), lambda s, g, c: (s, g, 0, 0))],
        out_shape=[jax.ShapeDtypeStruct((n, d_inner), BF16),
                   jax.ShapeDtypeStruct((n_seq,) + h0.shape[1:], F32)],
        scratch_shapes=[pltpu.VMEM((d_state, gw), F32)],
        compiler_params=_cp("parallel", "parallel", "arbitrary"),
        name="ssd_scan",
    )(xbc, xbc, xbc, z, dt_raw, dtb, a, d_e, ng, h0)
    return y, h_new


def _lanes(v, w):
    return jnp.concatenate([v] * (w // LANES), axis=1) if w >= LANES else v[:, :w]


def _softmax_step(s, m_ref, l_ref, acc_ref, vb):
    m_old = m_ref[...]
    m_new = jnp.maximum(m_old, jnp.max(s, axis=-1, keepdims=True))
    p = jnp.exp2(s - _lanes(m_new, s.shape[1]))
    corr = jnp.exp2(m_old - m_new)
    l_ref[...] = corr * l_ref[...] + jnp.sum(p, axis=-1, keepdims=True)
    acc_ref[...] = _lanes(corr, acc_ref.shape[1]) * acc_ref[...] + _dot(p.astype(BF16), vb)
    m_ref[...] = m_new


def _diff_finish(lam, lam_init, g_ref, o_ref, l0, a0, l1, a1):
    w = a0.shape[1]
    o = a0[...] / _lanes(l0[...], w) - lam * (a1[...] / _lanes(l1[...], w))
    ms = jnp.mean(o * o, axis=-1, keepdims=True)
    o_ref[...] = (o * lax.rsqrt(ms + RMS_EPS) * g_ref[...] * (1.0 - lam_init)).astype(o_ref.dtype)


def _diff_init(m0, l0, a0, m1, l1, a1):
    for m, l, a in ((m0, l0, a0), (m1, l1, a1)):
        m[...] = jnp.full_like(m, -jnp.inf)
        l[...] = jnp.zeros_like(l)
        a[...] = jnp.zeros_like(a)


def _diff_prompt_kernel(lam_ref, q_ref, k_ref, v_ref, g_ref, o_ref, m0, l0, a0, m1, l1, a1,
                        *, tq, tk, hd, lam_init):
    qi = pl.program_id(2)
    _diff_init(m0, l0, a0, m1, l1, a1)
    q = q_ref[...]
    q0, q1 = q[:, :hd], q[:, hd:]
    per_q = tq // tk

    def block(ki, mask):
        off = pl.multiple_of(ki * tk, tk)
        kb = k_ref[pl.ds(off, tk), :]
        vb = v_ref[pl.ds(off, tk), :]
        for qj, kj, m, l, a in ((q0, kb[:, :hd], m0, l0, a0), (q1, kb[:, hd:], m1, l1, a1)):
            s = _dot_nt(qj, kj)
            if mask is not None:
                s = jnp.where(mask, s, -jnp.inf)
            _softmax_step(s, m, l, a, vb)

    def body(ki, carry):
        block(ki, None)
        return carry

    lax.fori_loop(0, qi * per_q, body, 0)
    rc = lax.broadcasted_iota(jnp.int32, (tq, tk), 0) // MASK_CHUNK
    cc = lax.broadcasted_iota(jnp.int32, (tq, tk), 1) // MASK_CHUNK
    for d in range(per_q):
        block(qi * per_q + d, cc + d * (tk // MASK_CHUNK) <= rc)
    _diff_finish(lam_ref[0], lam_init, g_ref, o_ref, l0, a0, l1, a1)


def diff_attn_prompt(q, k, v, lam, g, n_seq, seq, heads, lam_init):
    n, d = q.shape
    hw = d // heads
    hd = hw // 2
    tq = _tile(seq, TQ_DIFF, MASK_CHUNK)
    tk = _tile(tq, TK_DIFF, MASK_CHUNK)
    qb = seq // tq
    stat = pltpu.VMEM((tq, LANES), F32)
    acc = pltpu.VMEM((tq, hw), F32)
    return pl.pallas_call(
        functools.partial(_diff_prompt_kernel, tq=tq, tk=tk, hd=hd, lam_init=lam_init),
        grid=(n_seq, heads, qb),
        in_specs=[pl.BlockSpec(memory_space=pltpu.SMEM),
                  pl.BlockSpec((tq, hw), lambda b, h, i: (b * qb + i, h)),
                  pl.BlockSpec((seq, hw), lambda b, h, i: (b, h)),
                  pl.BlockSpec((seq, hw), lambda b, h, i: (b, h)),
                  pl.BlockSpec((1, hw), lambda b, h, i: (0, 0))],
        out_specs=pl.BlockSpec((tq, hw), lambda b, h, i: (b * qb + i, h)),
        out_shape=jax.ShapeDtypeStruct((n, d), BF16),
        scratch_shapes=[stat, stat, acc, stat, stat, acc],
        compiler_params=_cp("parallel", "parallel", "arbitrary"),
        name="diff_attn_prompt",
    )(lam, q, k, v, g.reshape(1, hw))


def _diff_sample_kernel(lam_ref, q_ref, kc_ref, vc_ref, kn_ref, vn_ref, g_ref, o_ref, m_ref, l_ref, a_ref,
                        *, heads, nkc, hd, lam_init):
    c = pl.program_id(1)
    hw = 2 * hd
    halves = hw // LANES

    @pl.when(c == 0)
    def _():
        m_ref[...] = jnp.full_like(m_ref, -jnp.inf)
        l_ref[...] = jnp.zeros_like(l_ref)
        a_ref[...] = jnp.zeros_like(a_ref)

    def step(h, j, kb, vb):
        i = 2 * h + j
        qj = q_ref[:, i * hd:(i + 1) * hd]
        _softmax_step(_dot_nt(qj, kb), m_ref.at[i], l_ref.at[i], a_ref.at[i], vb)

    @pl.when(c < nkc)
    def _():
        for h in range(heads):
            vb = jnp.concatenate([vc_ref[:, e * heads + h, :] for e in range(halves)], axis=1).astype(BF16)
            for j in (0, 1):
                step(h, j, kc_ref[:, 2 * h + j, :].astype(BF16), vb)

    @pl.when(c == nkc)
    def _():
        for h in range(heads):
            vb = vn_ref[:, h * hw:(h + 1) * hw]
            for j in (0, 1):
                step(h, j, kn_ref[:, (2 * h + j) * hd:(2 * h + j + 1) * hd], vb)
            _diff_finish(lam_ref[0], lam_init, g_ref, o_ref.at[:, h * hw:(h + 1) * hw],
                         l_ref.at[2 * h], a_ref.at[2 * h], l_ref.at[2 * h + 1], a_ref.at[2 * h + 1])


def diff_attn_sample(q, k_new, v_new, k_cache, v_cache, cache_off, past, lam, g, n_seq, seq, heads, lam_init):
    n, d = q.shape
    hw = d // heads
    hd = hw // 2
    nch = d // LANES
    tk = _tile(past, TK_DIFF, SUBLANES)
    nkc = past // tk
    rows = pl.BlockSpec((seq, d), lambda b, c: (b, 0))
    cache = pl.BlockSpec((tk, nch, LANES), lambda b, c: ((cache_off + b) * nkc + jnp.minimum(c, nkc - 1), 0, 0))
    return pl.pallas_call(
        functools.partial(_diff_sample_kernel, heads=heads, nkc=nkc, hd=hd, lam_init=lam_init),
        grid=(n_seq, nkc + 1),
        in_specs=[pl.BlockSpec(memory_space=pltpu.SMEM), rows, cache, cache, rows, rows,
                  pl.BlockSpec((1, hw), lambda b, c: (0, 0))],
        out_specs=rows,
        out_shape=jax.ShapeDtypeStruct((n, d), BF16),
        scratch_shapes=[pltpu.VMEM((2 * heads, seq, LANES), F32), pltpu.VMEM((2 * heads, seq, LANES), F32),
                        pltpu.VMEM((2 * heads, seq, hw), F32)],
        compiler_params=_cp("parallel", "arbitrary"),
        name="diff_attn_sample",
    )(lam, q, k_cache, v_cache, k_new, v_new, g.reshape(1, hw))


def _mem_attn_kernel(q_ref, k_ref, v_ref, o_ref, *, heads, hd, scale):
    for h in range(heads):
        sl = slice(h * hd, (h + 1) * hd)
        s = _dot_nt(q_ref[:, sl], k_ref[:, sl].astype(BF16)) * scale
        p = jnp.exp(s - jnp.max(s, axis=-1, keepdims=True))
        l = jnp.sum(p, axis=-1, keepdims=True)
        o = _dot(p.astype(BF16), v_ref[:, sl].astype(BF16)) / l
        o_ref[:, sl] = o.astype(o_ref.dtype)


def mem_attn(q, mk, mv, kv_off, m, n_seq, seq, heads):
    n, d = q.shape
    hd = d // heads
    tq = _tile(seq, TQ_MEM, 16)
    qb = seq // tq
    kv = pl.BlockSpec((m, d), lambda b, i: (kv_off + b, 0))
    return pl.pallas_call(
        functools.partial(_mem_attn_kernel, heads=heads, hd=hd, scale=hd ** -0.5),
        grid=(n_seq, qb),
        in_specs=[pl.BlockSpec((tq, d), lambda b, i: (b * qb + i, 0)), kv, kv],
        out_specs=pl.BlockSpec((tq, d), lambda b, i: (b * qb + i, 0)),
        out_shape=jax.ShapeDtypeStruct((n, d), BF16),
        compiler_params=_cp("parallel", "parallel"),
        name="mem_attn",
    )(q, mk, mv)


def _top_rows(v, k):
    rows = []
    for _ in range(k):
        m = jnp.max(v, axis=0, keepdims=True)
        rows.append(m)
        v = jnp.where(v == m, -jnp.inf, v)
    return rows


def _peer_prep_kernel(xt_ref, wq_ref, keys_ref, s0_ref, e0_ref, s1_ref, e1_ref, tau_ref, *, heads, gpc):
    k = PEER_TOPK
    qt = _dot(wq_ref[...], xt_ref[...]).astype(BF16)
    half = keys_ref.shape[2]
    for h in range(heads):
        sc = [_dot(keys_ref[2 * h + j], qt[(2 * h + j) * half:(2 * h + j + 1) * half]) for j in (0, 1)]
        a = _top_rows(sc[0], k)
        b = jnp.concatenate(_top_rows(sc[1], k), axis=0)
        cand = [a[0] + b] + [a[r] + b[:k // 2] for r in range(1, k // 2)]
        cand.append(jnp.concatenate(a[k // 2:], axis=0) + b[0:1])
        cand = jnp.concatenate(cand, axis=0)
        tau = _top_rows(cand, k)[-1]
        mx = a[0] + b[0:1]
        z = jnp.sum(jnp.where(cand >= tau, jnp.exp(cand - mx), 0.0), axis=0, keepdims=True)
        e0 = jnp.exp(sc[0] - a[0])
        for c in range(s0_ref.shape[1]):
            s0_ref[h, c] = sc[0][c * gpc:(c + 1) * gpc]
            e0_ref[h, c] = e0[c * gpc:(c + 1) * gpc]
        s1_ref[h] = sc[1]
        e1_ref[h] = jnp.exp(sc[1] - b[0:1]) / z
        tau_ref[h:h + 1, :] = tau


def peer_prep(xt, wq_t, keys, gpc):
    d, n = xt.shape
    hj, nk, half = keys.shape
    heads = hj // 2
    assert PEER_TOPK % 2 == 0 and nk % gpc == 0
    t = _tile(n, T_PEER, LANES)
    first = pl.BlockSpec((heads, nk // gpc, gpc, t), lambda i: (0, 0, 0, i))
    second = pl.BlockSpec((heads, nk, t), lambda i: (0, 0, i))
    f_shape = jax.ShapeDtypeStruct((heads, nk // gpc, gpc, n), F32)
    s_shape = jax.ShapeDtypeStruct((heads, nk, n), F32)
    return pl.pallas_call(
        functools.partial(_peer_prep_kernel, heads=heads, gpc=gpc),
        grid=(n // t,),
        in_specs=[pl.BlockSpec((d, t), lambda i: (0, i)),
                  pl.BlockSpec(wq_t.shape, lambda i: (0, 0)),
                  pl.BlockSpec(keys.shape, lambda i: (0, 0, 0))],
        out_specs=[first, first, second, second, pl.BlockSpec((heads, t), lambda i: (0, i))],
        out_shape=[f_shape, f_shape, s_shape, s_shape, jax.ShapeDtypeStruct((heads, n), F32)],
        compiler_params=_cp("parallel"),
        name="peer_prep",
    )(xt, wq_t, keys)


def _gelu_tanh(x):
    return 0.5 * x * (1.0 + jnp.tanh(math.sqrt(2.0 / math.pi) * (x + 0.044715 * (x * x * x))))


def _peer_mix(s0_ref, e0_ref, s1_ref, e1_ref, tau_ref, act_ref, h_ref, *, heads, nk, t):
    gpc = act_ref.shape[0] // nk
    jh = PEER_J_ROWS
    for tl in range(t // LANES):
        ls = slice(tl * LANES, (tl + 1) * LANES)
        for jb in range(nk // jh):
            js = slice(jb * jh, (jb + 1) * jh)
            gates = [jnp.zeros((jh, LANES), F32) for _ in range(gpc)]
            for h in range(heads):
                s1, e1, tau = s1_ref[h, js, ls], e1_ref[h, js, ls], tau_ref[h:h + 1, ls]
                for ii in range(gpc):
                    hit = s0_ref[h, 0, ii:ii + 1, ls] + s1 >= tau
                    gates[ii] = gates[ii] + jnp.where(hit, e0_ref[h, 0, ii:ii + 1, ls] * e1, 0.0)
            for ii in range(gpc):
                rows = slice(ii * nk + jb * jh, ii * nk + (jb + 1) * jh)
                h_ref[rows, ls] = (gates[ii] * _gelu_tanh(act_ref[rows, ls])).astype(BF16)


def _peer_main_kernel(xt_ref, u_ref, vt_ref, s0a_ref, e0a_ref, s0b_ref, e0b_ref, s1_ref, e1_ref, tau_ref,
                      x_ref, lg_ref, lb_ref, o_ref, acc_ref, act_a, act_b, h_a, h_b,
                      *, heads, nk, ns, t, alpha):
    g = pl.program_id(1)
    ec = act_a.shape[0]
    mix = functools.partial(_peer_mix, s1_ref=s1_ref, e1_ref=e1_ref, tau_ref=tau_ref, heads=heads, nk=nk, t=t)

    @pl.when(g == 0)
    def _():
        acc_ref[...] = jnp.zeros_like(acc_ref)
        act_b[...] = jnp.zeros_like(act_b)
        h_a[...] = jnp.zeros_like(h_a)
        h_b[...] = jnp.zeros_like(h_b)

    xt = xt_ref[...]
    out_a = _dot(vt_ref[:, :ec], h_a[...])
    mix(s0b_ref, e0b_ref, act_ref=act_b, h_ref=h_b)
    act_a[...] = _dot(u_ref[:ec, :], xt)
    out_b = _dot(vt_ref[:, ec:], h_b[...])
    mix(s0a_ref, e0a_ref, act_ref=act_a, h_ref=h_a)
    act_b[...] = _dot(u_ref[ec:, :], xt)
    acc_ref[...] += out_a + out_b

    @pl.when(g == ns - 1)
    def _():
        o_ref[...] = _layer_norm(alpha * x_ref[...] + acc_ref[...].T, lg_ref[...], lb_ref[...])


def peer_main(xt, u, vt, s0, e0, s1, e1, tau, x, ln_g, ln_b, alpha):
    d, n = xt.shape
    heads, nchunk, gpc, _ = s0.shape
    nk = s1.shape[1]
    n_exp = u.shape[0]
    ec = gpc * nk
    assert n_exp == nchunk * ec and nchunk % 2 == 0
    t = _tile(n, T_PEER, LANES)
    ns = nchunk // 2 + 1
    last = nchunk // 2 - 1
    first_a = pl.BlockSpec((heads, 1, gpc, t), lambda i, g: (0, jnp.minimum(2 * g, nchunk - 1), 0, i))
    first_b = pl.BlockSpec((heads, 1, gpc, t), lambda i, g: (0, jnp.maximum(2 * g - 1, 0), 0, i))
    once = pl.Buffered(1)
    second = pl.BlockSpec((heads, nk, t), lambda i, g: (0, 0, i), pipeline_mode=once)
    return pl.pallas_call(
        functools.partial(_peer_main_kernel, heads=heads, nk=nk, ns=ns, t=t, alpha=alpha),
        grid=(n // t, ns),
        in_specs=[pl.BlockSpec((d, t), lambda i, g: (0, i)),
                  pl.BlockSpec((2 * ec, d), lambda i, g: (jnp.minimum(g, last), 0)),
                  pl.BlockSpec((d, 2 * ec), lambda i, g: (0, jnp.maximum(g - 1, 0))),
                  first_a, first_a, first_b, first_b, second, second,
                  pl.BlockSpec((heads, t), lambda i, g: (0, i)),
                  pl.BlockSpec((t, d), lambda i, g: (i, 0), pipeline_mode=once),
                  pl.BlockSpec((1, d), lambda i, g: (0, 0)),
                  pl.BlockSpec((1, d), lambda i, g: (0, 0))],
        out_specs=pl.BlockSpec((t, d), lambda i, g: (i, 0)),
        out_shape=jax.ShapeDtypeStruct((n, d), F32),
        scratch_shapes=[pltpu.VMEM((d, t), F32), pltpu.VMEM((ec, t), F32), pltpu.VMEM((ec, t), F32),
                        pltpu.VMEM((ec, t), BF16), pltpu.VMEM((ec, t), BF16)],
        compiler_params=_cp("parallel", "arbitrary"),
        name="peer_main",
    )(xt, u, vt, s0, e0, s0, e0, s1, e1, tau, x, ln_g.reshape(1, d), ln_b.reshape(1, d))


def _diff_lambda(lam_vecs, layer_idx):
    lam_init = 0.8 - 0.6 * math.exp(-0.3 * layer_idx)
    lv = lam_vecs.astype(F32)
    lam = jnp.exp(jnp.sum(lv[0] * lv[1])) - jnp.exp(jnp.sum(lv[2] * lv[3])) + lam_init
    return lam.reshape(1), lam_init


def kernel(x_prompt, x_sample, state_ssd_conv, state_ssd_h, cache_diff_k, cache_diff_v, cache_mem_k, cache_mem_v, mem_prompt, ln_g, ln_b, ssd_w_in, ssd_conv_w, ssd_conv_b, ssd_a_log, ssd_dt_bias, ssd_d, ssd_norm_g, ssd_w_out, diff_w_qkv, diff_lam, diff_subln_g, diff_w_o, mem_w_q, mem_w_kv, mem_w_o, peer_w_q, peer_keys, peer_u, peer_v):
    bp, lp, d = x_prompt.shape
    bs, ls, _ = x_sample.shape
    depth = ln_g.shape[0]
    alpha = (2.0 * depth) ** 0.25
    n_mixers = 2
    ssd_heads = ssd_a_log.shape[1]
    d_inner = ssd_norm_g.shape[1]
    conv_dim = ssd_conv_w.shape[2]
    d_state = state_ssd_h.shape[-1]
    groups = (conv_dim - d_inner) // (2 * d_state)
    hpg = ssd_heads // groups
    hd_ssd = d_inner // ssd_heads
    diff_heads = cache_diff_k.shape[3]
    mem_tokens, mem_heads = cache_mem_k.shape[2], cache_mem_k.shape[3]
    peer_heads, _, n_keys, peer_half = peer_keys.shape[1:]

    streams = [
        dict(x=x_prompt.reshape(bp * lp, d), n_seq=bp, seq=lp),
        dict(x=x_sample.reshape(bs * ls, d), n_seq=bs, seq=ls),
    ]
    outs = [dict(conv=[], h=[], dk=[], dv=[]), dict(conv=[], h=[], dk=[], dv=[])]
    mk_p, mv_p = [], []

    def group_lanes(v):
        v = v.reshape(v.shape[:-1] + (groups, hpg))
        v = jnp.pad(v, [(0, 0)] * (v.ndim - 1) + [(0, LANES - hpg)])
        return v.reshape(v.shape[:-2] + (groups * LANES,))

    for i in range(depth):
        j = i // n_mixers
        g0, b0 = ln_g[i, 0], ln_b[i, 0]
        if i % n_mixers == 0:
            w_in = ssd_w_in[j]
            w_z = w_in[:, :d_inner].astype(BF16)
            w_xbc = w_in[:, d_inner:d_inner + conv_dim].astype(BF16)
            w_dt = group_lanes(w_in[:, d_inner + conv_dim:]).astype(BF16)
            dtb = group_lanes(ssd_dt_bias[j]).reshape(1, -1)
            a_neg = group_lanes(-jnp.exp(ssd_a_log[j].astype(F32))).reshape(1, -1)
            d_e = jnp.repeat(ssd_d[j], hd_ssd).reshape(1, d_inner)
            ng = ssd_norm_g[j].reshape(1, d_inner)
            w_out = ssd_w_out[j].astype(BF16)
            for si, st in enumerate(streams):
                x, n_seq, seq = st["x"], st["n_seq"], st["seq"]
                z = matmul(x, w_z)
                xbc = matmul(x, w_xbc)
                dt_raw = matmul(x, w_dt)
                if si == 0:
                    state8 = jnp.zeros((n_seq, SUBLANES, conv_dim), F32)
                    h0, h0_off = jnp.zeros((n_seq, ssd_heads, hd_ssd, d_state), F32), 0
                else:
                    state8 = jnp.pad(state_ssd_conv[j], ((0, 0), (SUBLANES - (CONV_W - 1), 0), (0, 0)))
                    h0, h0_off = state_ssd_h.reshape((-1,) + state_ssd_h.shape[2:]), j * n_seq
                outs[si]["conv"].append(xbc.reshape(n_seq, seq, conv_dim)[:, seq - (CONV_W - 1):])
                xc = conv_silu(xbc, state8, ssd_conv_w[j], ssd_conv_b[j], n_seq, seq)
                y, h_new = ssd_scan(xc, z, dt_raw, dtb, a_neg, d_e, ng, h0, h0_off, n_seq, seq,
                                    d_inner, d_state, groups)
                outs[si]["h"].append(h_new)
                st["x"] = matmul_ln(y, w_out, x, g0, b0, alpha)
        else:
            lam, lam_init = _diff_lambda(diff_lam[j], i)
            w_qkv = diff_w_qkv[j]
            w_q, w_k, w_v = (w_qkv[:, c * d:(c + 1) * d].astype(BF16) for c in range(3))
            w_o = diff_w_o[j].astype(BF16)
            for si, st in enumerate(streams):
                x, n_seq, seq = st["x"], st["n_seq"], st["seq"]
                hd_diff = d // diff_heads // 2
                q = matmul(x, w_q, (BF16,), scale=hd_diff ** -0.5 * math.log2(math.e))
                hw = d // diff_heads
                halves = hw // LANES
                k32, k16 = matmul_rows(x, w_k, range(d // LANES))
                v32, v16 = matmul_rows(x, w_v, [(c % halves) * diff_heads + c // halves for c in range(d // LANES)])
                outs[si]["dk"].append(k32.reshape(n_seq, seq, diff_heads, 2, hw // 2))
                outs[si]["dv"].append(v32.reshape(n_seq, seq, halves, diff_heads, LANES)
                                      .transpose(0, 1, 3, 2, 4).reshape(n_seq, seq, diff_heads, hw))
                if si == 0:
                    o = diff_attn_prompt(q, k16, v16, lam, diff_subln_g[j], n_seq, seq, diff_heads, lam_init)
                else:
                    past = cache_diff_k.shape[2]
                    kc = cache_diff_k.reshape(-1, d // LANES, LANES)
                    vc = (cache_diff_v.reshape(-1, diff_heads, halves, LANES).transpose(0, 2, 1, 3)
                          .reshape(-1, d // LANES, LANES))
                    o = diff_attn_sample(q, k16, v16, kc, vc, j * n_seq, past, lam, diff_subln_g[j],
                                         n_seq, seq, diff_heads, lam_init)
                st["x"] = matmul_ln(o, w_o, x, g0, b0, alpha)

        w_mq = mem_w_q[i].astype(BF16)
        w_mk = mem_w_kv[i][:, :d].astype(BF16)
        w_mv = mem_w_kv[i][:, d:].astype(BF16)
        w_mo = mem_w_o[i].astype(BF16)
        mem_rows = mem_prompt.reshape(bp * mem_tokens, d)
        mk = matmul(mem_rows, w_mk)
        mv = matmul(mem_rows, w_mv)
        mk_p.append(mk.reshape(bp, mem_tokens, mem_heads, d // mem_heads))
        mv_p.append(mv.reshape(bp, mem_tokens, mem_heads, d // mem_heads))
        wq_t = peer_w_q[i].T.astype(BF16)
        keys = peer_keys[i].reshape(peer_heads * 2, n_keys, peer_half).astype(BF16)
        u = peer_u[i].astype(BF16)
        vt = peer_v[i].T.astype(BF16)
        for si, st in enumerate(streams):
            x, n_seq, seq = st["x"], st["n_seq"], st["seq"]
            q = matmul(x, w_mq, (BF16,))
            if si == 0:
                kk, vv, kv_off = mk, mv, 0
            else:
                kk, vv, kv_off = cache_mem_k.reshape(-1, d), cache_mem_v.reshape(-1, d), i * n_seq
            o = mem_attn(q, kk, vv, kv_off, mem_tokens, n_seq, seq, mem_heads)
            x, xt = matmul_ln(o, w_mo, x, ln_g[i, 1], ln_b[i, 1], alpha, with_t=True)
            s0, e0, s1, e1, tau = peer_prep(xt, wq_t, keys, EC_PEER // n_keys)
            st["x"] = peer_main(xt, u, vt, s0, e0, s1, e1, tau, x, ln_g[i, 2], ln_b[i, 2], alpha)

    yp = streams[0]["x"].reshape(bp, lp, d)
    ys = streams[1]["x"].reshape(bs, ls, d)
    o_p, o_s = outs
    return (yp, ys, jnp.stack(o_p["conv"]), jnp.stack(o_p["h"]), jnp.stack(o_p["dk"]), jnp.stack(o_p["dv"]),
            jnp.stack(mk_p), jnp.stack(mv_p), jnp.stack(o_s["conv"]), jnp.stack(o_s["h"]),
            jnp.stack(o_s["dk"]), jnp.stack(o_s["dv"]))
```

```python
import functools
import math

import jax
import jax.numpy as jnp
from jax import lax
from jax.experimental import pallas as pl
from jax.experimental.pallas import tpu as pltpu

F32 = jnp.float32
BF16 = jnp.bfloat16

LANES = 128
SUBLANES = 8
VMEM_LIMIT_BYTES = 56 * 1024 * 1024

MASK_CHUNK = 64
CONV_W = 4
PEER_TOPK = 16
LN_EPS = 1e-5
RMS_EPS = 1e-5
SSD_CHUNK = 128
SSD_GROUPS_PER_STEP = 2

TM = 1024
TN = 512
TM_LN = 512
TK = 2048
TQ_DIFF = 1024
TK_DIFF = 512
TQ_MEM = 512
T_PEER = 512
EC_PEER = 512
PEER_J_ROWS = 32

_NT = (((1,), (1,)), ((), ()))


def _tile(n, target, mult):
    t = (min(target, n) // mult) * mult
    while t >= mult:
        if n % t == 0:
            return t
        t -= mult
    return n


def _cp(*sem):
    return pltpu.CompilerParams(dimension_semantics=sem, vmem_limit_bytes=VMEM_LIMIT_BYTES)


def _dot(a, b):
    return jnp.dot(a, b, preferred_element_type=F32)


def _dot_nt(a, b):
    return lax.dot_general(a, b, _NT, preferred_element_type=F32)


def _layer_norm(h, g, b):
    mu = jnp.mean(h, axis=-1, keepdims=True)
    d = h - mu
    var = jnp.mean(d * d, axis=-1, keepdims=True)
    return d * lax.rsqrt(var + LN_EPS) * g + b


def _mm_kernel(a_ref, w_ref, *o_refs, scale):
    acc = _dot(a_ref[...].astype(BF16), w_ref[...])
    if scale is not None:
        acc = acc * scale
    for o in o_refs:
        o[...] = acc.astype(o.dtype)


def matmul(a, w, out_dtypes=(F32,), scale=None):
    n, k = a.shape
    m = w.shape[1]
    tm, tn = _tile(n, TM, 16), _tile(m, TN, LANES)
    outs = pl.pallas_call(
        functools.partial(_mm_kernel, scale=scale),
        grid=(n // tm, m // tn),
        in_specs=[pl.BlockSpec((tm, k), lambda i, j: (i, 0)),
                  pl.BlockSpec((k, tn), lambda i, j: (0, j))],
        out_specs=[pl.BlockSpec((tm, tn), lambda i, j: (i, j)) for _ in out_dtypes],
        out_shape=[jax.ShapeDtypeStruct((n, m), d) for d in out_dtypes],
        compiler_params=_cp("parallel", "parallel"),
        name="matmul",
    )(a, w)
    return outs[0] if len(out_dtypes) == 1 else outs


def _mm_rows_kernel(a_ref, w_ref, lin_ref, o16_ref, *, tm, order):
    acc = _dot(a_ref[...].astype(BF16), w_ref[...])
    o16_ref[...] = acc.astype(BF16)
    nchunk = len(order)
    for c, r in enumerate(order):
        lin_ref[pl.ds(r, tm, stride=nchunk), :] = acc[:, c * LANES:(c + 1) * LANES]


def matmul_rows(a, w, order):
    n, k = a.shape
    m = w.shape[1]
    nchunk = m // LANES
    tm = _tile(n, TM_LN, 16)
    lin, o16 = pl.pallas_call(
        functools.partial(_mm_rows_kernel, tm=tm, order=tuple(order)),
        grid=(n // tm,),
        in_specs=[pl.BlockSpec((tm, k), lambda i: (i, 0)),
                  pl.BlockSpec((k, m), lambda i: (0, 0))],
        out_specs=[pl.BlockSpec((tm * nchunk, LANES), lambda i: (i, 0)),
                   pl.BlockSpec((tm, m), lambda i: (i, 0))],
        out_shape=[jax.ShapeDtypeStruct((n * nchunk, LANES), F32), jax.ShapeDtypeStruct((n, m), BF16)],
        compiler_params=_cp("parallel"),
        name="matmul_rows",
    )(a, w)
    return lin, o16


def _mm_ln_kernel(a_ref, w_ref, x_ref, g_ref, b_ref, *refs, alpha, nk, with_t):
    o_ref = refs[0]
    acc_ref = refs[-1] if nk > 1 else None
    k = pl.program_id(1)
    part = _dot(a_ref[...].astype(BF16), w_ref[...])

    def finish(f):
        y = _layer_norm(alpha * x_ref[...] + f, g_ref[...], b_ref[...])
        o_ref[...] = y
        if with_t:
            refs[1][...] = y.T.astype(BF16)

    if nk == 1:
        finish(part)
        return

    @pl.when(k == 0)
    def _():
        acc_ref[...] = part

    @pl.when(jnp.logical_and(k > 0, k < nk - 1))
    def _():
        acc_ref[...] += part

    @pl.when(k == nk - 1)
    def _():
        finish(acc_ref[...] + part)


def matmul_ln(a, w, x, g, b, alpha, with_t=False):
    n, kdim = a.shape
    d = w.shape[1]
    tm, tk = _tile(n, TM_LN, LANES if with_t else 16), _tile(kdim, TK, LANES)
    nk = kdim // tk
    out_shape = [jax.ShapeDtypeStruct((n, d), F32)]
    out_specs = [pl.BlockSpec((tm, d), lambda i, k: (i, 0))]
    if with_t:
        out_shape.append(jax.ShapeDtypeStruct((d, n), BF16))
        out_specs.append(pl.BlockSpec((d, tm), lambda i, k: (0, i)))
    outs = pl.pallas_call(
        functools.partial(_mm_ln_kernel, alpha=alpha, nk=nk, with_t=with_t),
        grid=(n // tm, nk),
        in_specs=[pl.BlockSpec((tm, tk), lambda i, k: (i, k)),
                  pl.BlockSpec((tk, d), lambda i, k: (k, 0)),
                  pl.BlockSpec((tm, d), lambda i, k: (i, 0)),
                  pl.BlockSpec((1, d), lambda i, k: (0, 0)),
                  pl.BlockSpec((1, d), lambda i, k: (0, 0))],
        out_specs=out_specs,
        out_shape=out_shape,
        scratch_shapes=[pltpu.VMEM((tm, d), F32)] if nk > 1 else [],
        compiler_params=_cp("parallel", "arbitrary"),
        name="matmul_ln",
    )(a, w, x, g.reshape(1, d), b.reshape(1, d))
    return outs if with_t else outs[0]


def _split_bf16(v, parts):
    out, r = [], v
    for _ in range(parts - 1):
        p = r.astype(BF16)
        out.append(p)
        r = r - p.astype(F32)
    out.append(r.astype(BF16))
    return out


def _expand(vals, width, hpg, parts):
    rows = vals[0].shape[0]
    lhs = jnp.concatenate([p for v in vals for p in _split_bf16(v, parts)], axis=0)
    k = lax.broadcasted_iota(jnp.int32, (LANES, hpg * width), 0)
    c = lax.broadcasted_iota(jnp.int32, (LANES, hpg * width), 1) // width
    res = _dot(lhs, jnp.where(k == c, 1.0, 0.0).astype(BF16))
    out = []
    for i in range(len(vals)):
        acc = res[i * parts * rows:(i * parts + 1) * rows]
        for j in range(1, parts):
            acc = acc + res[(i * parts + j) * rows:(i * parts + j + 1) * rows]
        out.append(acc)
    return out


def _ssd_kernel(x_ref, b_ref, c_ref, sx_ref, sb_ref, sc_ref, wx_ref, wb_ref, wc_ref, bx_ref, bb_ref, bc_ref,
                z_ref, dt_ref, dtb_ref, a_ref, d_ref, ng_ref, h0_ref,
                y_ref, hout_ref, ht_ref, kx_ref, kb_ref, kc_ref, *, rows, valid, nc, hpg, hd, gpb):
    c = pl.program_id(2)
    gw = hpg * hd
    ds = b_ref.shape[1] // gpb

    def pad(v):
        if valid == rows:
            return v
        return jnp.concatenate([v, jnp.zeros((rows - valid, v.shape[1]), v.dtype)], axis=0)

    def conv_silu(raw_ref, st_ref, w_ref, bias_ref, keep_ref):
        @pl.when(c == 0)
        def _():
            keep_ref[...] = st_ref[0]

        raw = raw_ref[...]
        ext = jnp.concatenate([keep_ref[...], raw], axis=0)
        w = w_ref[...]
        acc = bias_ref[...] + w[0:1] * ext[5:5 + valid]
        acc = acc + w[1:2] * ext[6:6 + valid]
        acc = acc + w[2:3] * ext[7:7 + valid]
        acc = acc + w[3:4] * raw
        keep_ref[...] = raw[valid - SUBLANES:]
        return pad(acc * jax.nn.sigmoid(acc))

    x_all = conv_silu(x_ref, sx_ref, wx_ref, bx_ref, kx_ref)
    b_all = conv_silu(b_ref, sb_ref, wb_ref, bb_ref, kb_ref)
    c_all = conv_silu(c_ref, sc_ref, wc_ref, bc_ref, kc_ref)

    ri = lax.broadcasted_iota(jnp.int32, (rows, rows), 0)
    ci = lax.broadcasted_iota(jnp.int32, (rows, rows), 1)
    tri = ri >= ci
    trif = tri.astype(F32)
    for gi in range(gpb):
        gs = slice(gi * gw, (gi + 1) * gw)
        ls = slice(gi * LANES, (gi + 1) * LANES)
        ss = slice(gi * ds, (gi + 1) * ds)

        @pl.when(c == 0)
        def _():
            ht_ref[gi] = h0_ref[0, gi * hpg:(gi + 1) * hpg].reshape(gw, ds).T

        x = x_all[:, gs]
        bm = b_all[:, ss]
        cm = c_all[:, ss]
        dt = pad(jax.nn.softplus(dt_ref[:, ls] + dtb_ref[:, ls]))
        da = dt * a_ref[:, ls]
        cum = jnp.dot(trif, da, precision=lax.Precision.HIGHEST,
                      preferred_element_type=F32)
        cum_t = cum.T
        ecum = jnp.exp(cum)
        toend = jnp.exp(cum[rows - 1:rows, :] - cum)
        dt_e, ecum_e, w_e = _expand([dt, ecum, dt * toend], hd, hpg, 2)
        (cum_e,) = _expand([cum], rows, hpg, 3)
        cb16 = cm.astype(BF16)
        cb = _dot_nt(cb16, bm.astype(BF16))
        ht = ht_ref[gi]
        yoff = _dot(cb16, ht.astype(BF16))
        xdt16 = (x * dt_e).astype(BF16)
        ys = []
        for h in range(hpg):
            dec = jnp.exp(jnp.where(tri, cum_e[:, h * rows:(h + 1) * rows] - cum_t[h:h + 1, :], -jnp.inf))
            ys.append(_dot((cb * dec).astype(BF16), xdt16[:, h * hd:(h + 1) * hd]))
        y = jnp.concatenate(ys, axis=1) + yoff * ecum_e
        ht_new = ht * ecum_e[rows - 1:rows, :] + _dot(bm.T.astype(BF16), (x * w_e).astype(BF16))
        ht_ref[gi] = ht_new

        z = pad(z_ref[:, gs])
        y = (y + x * d_ref[:, gs]) * (z * jax.nn.sigmoid(z))
        ms = jnp.mean(y * y, axis=-1, keepdims=True)
        yn = y * lax.rsqrt(ms + RMS_EPS) * ng_ref[:, gs]
        y_ref[:, gs] = yn[:valid].astype(y_ref.dtype)

        @pl.when(c == nc - 1)
        def _():
            hout_ref[0, gi * hpg:(gi + 1) * hpg] = ht_new.T.reshape((hpg,) + hout_ref.shape[2:])


def ssd_scan(xbc, state8, conv_w, conv_b, z, dt_raw, dtb, a, d_e, ng, h0, h0_off, n_seq, seq,
             d_inner, d_state, groups):
    n = xbc.shape[0]
    heads, hd = h0.shape[1], h0.shape[2]
    hpg = heads // groups
    gw = d_inner // groups
    gpb = SSD_GROUPS_PER_STEP
    assert gw == hpg * hd and d_state == LANES and gw % LANES == 0 and groups % gpb == 0
    rows = SSD_CHUNK
    valid = rows if seq % rows == 0 else seq
    assert valid == rows or seq < rows
    nc = seq // valid
    boff, coff = d_inner // (gpb * d_state), (d_inner // d_state + groups) // gpb
    row = lambda s, g, c: s * nc + c
    xw, bw = gpb * gw, gpb * d_state
    cmaps = ((xw, lambda g: g), (bw, lambda g: boff + g), (bw, lambda g: coff + g))

    def per_col(lead):
        return [pl.BlockSpec(tuple(lead) + (w,), lambda s, g, c, cm=cm, k=len(lead): (0,) * k + (cm(g),))
                for w, cm in cmaps]

    y, h_new = pl.pallas_call(
        functools.partial(_ssd_kernel, rows=rows, valid=valid, nc=nc, hpg=hpg, hd=hd, gpb=gpb),
        grid=(n_seq, groups // gpb, nc),
        in_specs=[pl.BlockSpec((valid, w), lambda s, g, c, cm=cm: (row(s, g, c), cm(g))) for w, cm in cmaps]
        + [pl.BlockSpec((1, SUBLANES, w), lambda s, g, c, cm=cm: (s, 0, cm(g))) for w, cm in cmaps]
        + per_col((CONV_W,)) + per_col((1,))
        + [pl.BlockSpec((valid, gpb * gw), lambda s, g, c: (row(s, g, c), g)),
                  pl.BlockSpec((valid, gpb * LANES), lambda s, g, c: (row(s, g, c), g)),
                  pl.BlockSpec((1, gpb * LANES), lambda s, g, c: (0, g)),
                  pl.BlockSpec((1, gpb * LANES), lambda s, g, c: (0, g)),
                  pl.BlockSpec((1, gpb * gw), lambda s, g, c: (0, g)),
                  pl.BlockSpec((1, gpb * gw), lambda s, g, c: (0, g)),
                  pl.BlockSpec((1, gpb * hpg, hd, d_state), lambda s, g, c: (h0_off + s, g, 0, 0))],
        out_specs=[pl.BlockSpec((valid, gpb * gw), lambda s, g, c: (row(s, g, c), g)),
                   pl.BlockSpec((1, gpb * hpg, hd, d_state), lambda s, g, c: (s, g, 0, 0))],
        out_shape=[jax.ShapeDtypeStruct((n, d_inner), BF16),
                   jax.ShapeDtypeStruct((n_seq,) + h0.shape[1:], F32)],
        scratch_shapes=[pltpu.VMEM((gpb, d_state, gw), F32), pltpu.VMEM((SUBLANES, xw), F32),
                        pltpu.VMEM((SUBLANES, bw), F32), pltpu.VMEM((SUBLANES, bw), F32)],
        compiler_params=_cp("parallel", "parallel", "arbitrary"),
        name="ssd_scan",
    )(xbc, xbc, xbc, state8, state8, state8, conv_w, conv_w, conv_w, conv_b, conv_b, conv_b,
      z, dt_raw, dtb, a, d_e, ng, h0)
    return y, h_new


def _lanes(v, w):
    return jnp.concatenate([v] * (w // LANES), axis=1) if w >= LANES else v[:, :w]


def _softmax_step(s, m_ref, l_ref, acc_ref, vb):
    m_old = m_ref[...]
    m_new = jnp.maximum(m_old, jnp.max(s, axis=-1, keepdims=True))
    p = jnp.exp2(s - _lanes(m_new, s.shape[1]))
    corr = jnp.exp2(m_old - m_new)
    l_ref[...] = corr * l_ref[...] + jnp.sum(p, axis=-1, keepdims=True)
    acc_ref[...] = _lanes(corr, acc_ref.shape[1]) * acc_ref[...] + _dot(p.astype(BF16), vb)
    m_ref[...] = m_new


def _diff_finish(lam, lam_init, g_ref, o_ref, l0, a0, l1, a1):
    w = a0.shape[1]
    o = a0[...] / _lanes(l0[...], w) - lam * (a1[...] / _lanes(l1[...], w))
    ms = jnp.mean(o * o, axis=-1, keepdims=True)
    o_ref[...] = (o * lax.rsqrt(ms + RMS_EPS) * g_ref[...] * (1.0 - lam_init)).astype(o_ref.dtype)


def _diff_init(m0, l0, a0, m1, l1, a1):
    for m, l, a in ((m0, l0, a0), (m1, l1, a1)):
        m[...] = jnp.full_like(m, -jnp.inf)
        l[...] = jnp.zeros_like(l)
        a[...] = jnp.zeros_like(a)


def _diff_prompt_kernel(lam_ref, q_ref, k_ref, v_ref, g_ref, o_ref, m0, l0, a0, m1, l1, a1,
                        *, tq, tk, hd, lam_init):
    qi = pl.program_id(2)
    _diff_init(m0, l0, a0, m1, l1, a1)
    q = q_ref[...]
    q0, q1 = q[:, :hd], q[:, hd:]
    per_q = tq // tk

    def block(ki, mask, r0=0):
        off = pl.multiple_of(ki * tk, tk)
        kb = k_ref[pl.ds(off, tk), :]
        vb = v_ref[pl.ds(off, tk), :]
        rows = pl.ds(r0, tq - r0)
        for qj, kj, m, l, a in ((q0, kb[:, :hd], m0, l0, a0), (q1, kb[:, hd:], m1, l1, a1)):
            s = _dot_nt(qj[r0:], kj)
            if mask is not None:
                s = jnp.where(mask, s, -jnp.inf)
            _softmax_step(s, m.at[rows], l.at[rows], a.at[rows], vb)

    def body(ki, carry):
        block(ki, None)
        return carry

    lax.fori_loop(0, qi * per_q, body, 0)
    for d in range(per_q):
        r0 = d * tk
        rc = (lax.broadcasted_iota(jnp.int32, (tq - r0, tk), 0) + r0) // MASK_CHUNK
        cc = (lax.broadcasted_iota(jnp.int32, (tq - r0, tk), 1) + r0) // MASK_CHUNK
        block(qi * per_q + d, cc <= rc, r0)
    _diff_finish(lam_ref[0], lam_init, g_ref, o_ref, l0, a0, l1, a1)


def diff_attn_prompt(q, k, v, lam, g, n_seq, seq, heads, lam_init):
    n, d = q.shape
    hw = d // heads
    hd = hw // 2
    tq = _tile(seq, TQ_DIFF, MASK_CHUNK)
    tk = _tile(tq, TK_DIFF, MASK_CHUNK)
    qb = seq // tq
    stat = pltpu.VMEM((tq, LANES), F32)
    acc = pltpu.VMEM((tq, hw), F32)
    return pl.pallas_call(
        functools.partial(_diff_prompt_kernel, tq=tq, tk=tk, hd=hd, lam_init=lam_init),
        grid=(n_seq, heads, qb),
        in_specs=[pl.BlockSpec(memory_space=pltpu.SMEM),
                  pl.BlockSpec((tq, hw), lambda b, h, i: (b * qb + i, h)),
                  pl.BlockSpec((seq, hw), lambda b, h, i: (b, h)),
                  pl.BlockSpec((seq, hw), lambda b, h, i: (b, h)),
                  pl.BlockSpec((1, hw), lambda b, h, i: (0, 0))],
        out_specs=pl.BlockSpec((tq, hw), lambda b, h, i: (b * qb + i, h)),
        out_shape=jax.ShapeDtypeStruct((n, d), BF16),
        scratch_shapes=[stat, stat, acc, stat, stat, acc],
        compiler_params=_cp("parallel", "parallel", "arbitrary"),
        name="diff_attn_prompt",
    )(lam, q, k, v, g.reshape(1, hw))


def _diff_sample_kernel(lam_ref, q_ref, kc_ref, vc_ref, kn_ref, vn_ref, g_ref, o_ref, m_ref, l_ref, a_ref,
                        *, heads, nkc, hd, lam_init):
    c = pl.program_id(1)
    hw = 2 * hd
    halves = hw // LANES

    @pl.when(c == 0)
    def _():
        m_ref[...] = jnp.full_like(m_ref, -jnp.inf)
        l_ref[...] = jnp.zeros_like(l_ref)
        a_ref[...] = jnp.zeros_like(a_ref)

    def step(h, j, kb, vb):
        i = 2 * h + j
        qj = q_ref[:, i * hd:(i + 1) * hd]
        _softmax_step(_dot_nt(qj, kb), m_ref.at[i], l_ref.at[i], a_ref.at[i], vb)

    @pl.when(c < nkc)
    def _():
        for h in range(heads):
            vb = jnp.concatenate([vc_ref[:, e * heads + h, :] for e in range(halves)], axis=1).astype(BF16)
            for j in (0, 1):
                step(h, j, kc_ref[:, 2 * h + j, :].astype(BF16), vb)

    @pl.when(c == nkc)
    def _():
        for h in range(heads):
            vb = vn_ref[:, h * hw:(h + 1) * hw]
            for j in (0, 1):
                step(h, j, kn_ref[:, (2 * h + j) * hd:(2 * h + j + 1) * hd], vb)
            _diff_finish(lam_ref[0], lam_init, g_ref, o_ref.at[:, h * hw:(h + 1) * hw],
                         l_ref.at[2 * h], a_ref.at[2 * h], l_ref.at[2 * h + 1], a_ref.at[2 * h + 1])


def diff_attn_sample(q, k_new, v_new, k_cache, v_cache, cache_off, past, lam, g, n_seq, seq, heads, lam_init):
    n, d = q.shape
    hw = d // heads
    hd = hw // 2
    nch = d // LANES
    tk = _tile(past, TK_DIFF, SUBLANES)
    nkc = past // tk
    rows = pl.BlockSpec((seq, d), lambda b, c: (b, 0))
    cache = pl.BlockSpec((tk, nch, LANES), lambda b, c: ((cache_off + b) * nkc + jnp.minimum(c, nkc - 1), 0, 0))
    return pl.pallas_call(
        functools.partial(_diff_sample_kernel, heads=heads, nkc=nkc, hd=hd, lam_init=lam_init),
        grid=(n_seq, nkc + 1),
        in_specs=[pl.BlockSpec(memory_space=pltpu.SMEM), rows, cache, cache, rows, rows,
                  pl.BlockSpec((1, hw), lambda b, c: (0, 0))],
        out_specs=rows,
        out_shape=jax.ShapeDtypeStruct((n, d), BF16),
        scratch_shapes=[pltpu.VMEM((2 * heads, seq, LANES), F32), pltpu.VMEM((2 * heads, seq, LANES), F32),
                        pltpu.VMEM((2 * heads, seq, hw), F32)],
        compiler_params=_cp("parallel", "arbitrary"),
        name="diff_attn_sample",
    )(lam, q, k_cache, v_cache, k_new, v_new, g.reshape(1, hw))


def _mem_attn_kernel(q_ref, k_ref, v_ref, o_ref, *, heads, hd, scale):
    for h in range(heads):
        sl = slice(h * hd, (h + 1) * hd)
        s = _dot_nt(q_ref[:, sl], k_ref[:, sl].astype(BF16)) * scale
        p = jnp.exp(s - jnp.max(s, axis=-1, keepdims=True))
        l = jnp.sum(p, axis=-1, keepdims=True)
        o = _dot(p.astype(BF16), v_ref[:, sl].astype(BF16)) / l
        o_ref[:, sl] = o.astype(o_ref.dtype)


def mem_attn(q, mk, mv, kv_off, m, n_seq, seq, heads):
    n, d = q.shape
    hd = d // heads
    tq = _tile(seq, TQ_MEM, 16)
    qb = seq // tq
    kv = pl.BlockSpec((m, d), lambda b, i: (kv_off + b, 0))
    return pl.pallas_call(
        functools.partial(_mem_attn_kernel, heads=heads, hd=hd, scale=hd ** -0.5),
        grid=(n_seq, qb),
        in_specs=[pl.BlockSpec((tq, d), lambda b, i: (b * qb + i, 0)), kv, kv],
        out_specs=pl.BlockSpec((tq, d), lambda b, i: (b * qb + i, 0)),
        out_shape=jax.ShapeDtypeStruct((n, d), BF16),
        compiler_params=_cp("parallel", "parallel"),
        name="mem_attn",
    )(q, mk, mv)


def _top_rows(v, k):
    rows = []
    for _ in range(k):
        m = jnp.max(v, axis=0, keepdims=True)
        rows.append(m)
        v = jnp.where(v == m, -jnp.inf, v)
    return rows


def _peer_prep_kernel(xt_ref, wq_ref, keys_ref, s0_ref, e0_ref, s1_ref, e1_ref, tau_ref, *, heads, gpc):
    k = PEER_TOPK
    qt = _dot(wq_ref[...], xt_ref[...]).astype(BF16)
    half = keys_ref.shape[2]
    for h in range(heads):
        sc = [_dot(keys_ref[2 * h + j], qt[(2 * h + j) * half:(2 * h + j + 1) * half]) for j in (0, 1)]
        a = _top_rows(sc[0], k)
        b = jnp.concatenate(_top_rows(sc[1], k), axis=0)
        cand = [a[0] + b] + [a[r] + b[:k // 2] for r in range(1, k // 2)]
        cand.append(jnp.concatenate(a[k // 2:], axis=0) + b[0:1])
        cand = jnp.concatenate(cand, axis=0)
        tau = _top_rows(cand, k)[-1]
        mx = a[0] + b[0:1]
        z = jnp.sum(jnp.where(cand >= tau, jnp.exp(cand - mx), 0.0), axis=0, keepdims=True)
        e0 = jnp.exp(sc[0] - a[0])
        for c in range(s0_ref.shape[1]):
            s0_ref[h, c] = sc[0][c * gpc:(c + 1) * gpc]
            e0_ref[h, c] = e0[c * gpc:(c + 1) * gpc]
        s1_ref[h] = sc[1]
        e1_ref[h] = jnp.exp(sc[1] - b[0:1]) / z
        tau_ref[h:h + 1, :] = tau


def peer_prep(xt, wq_t, keys, gpc):
    d, n = xt.shape
    hj, nk, half = keys.shape
    heads = hj // 2
    assert PEER_TOPK % 2 == 0 and nk % gpc == 0
    t = _tile(n, T_PEER, LANES)
    first = pl.BlockSpec((heads, nk // gpc, gpc, t), lambda i: (0, 0, 0, i))
    second = pl.BlockSpec((heads, nk, t), lambda i: (0, 0, i))
    f_shape = jax.ShapeDtypeStruct((heads, nk // gpc, gpc, n), F32)
    s_shape = jax.ShapeDtypeStruct((heads, nk, n), F32)
    return pl.pallas_call(
        functools.partial(_peer_prep_kernel, heads=heads, gpc=gpc),
        grid=(n // t,),
        in_specs=[pl.BlockSpec((d, t), lambda i: (0, i)),
                  pl.BlockSpec(wq_t.shape, lambda i: (0, 0)),
                  pl.BlockSpec(keys.shape, lambda i: (0, 0, 0))],
        out_specs=[first, first, second, second, pl.BlockSpec((heads, t), lambda i: (0, i))],
        out_shape=[f_shape, f_shape, s_shape, s_shape, jax.ShapeDtypeStruct((heads, n), F32)],
        compiler_params=_cp("parallel"),
        name="peer_prep",
    )(xt, wq_t, keys)


def _gelu_tanh(x):
    return 0.5 * x * (1.0 + jnp.tanh(math.sqrt(2.0 / math.pi) * (x + 0.044715 * (x * x * x))))


def _peer_mix(s0_ref, e0_ref, s1_ref, e1_ref, tau_ref, act_ref, h_ref, *, heads, nk, t):
    gpc = act_ref.shape[0] // nk
    jh = PEER_J_ROWS
    for tl in range(t // LANES):
        ls = slice(tl * LANES, (tl + 1) * LANES)
        for jb in range(nk // jh):
            js = slice(jb * jh, (jb + 1) * jh)
            gates = [jnp.zeros((jh, LANES), F32) for _ in range(gpc)]
            for h in range(heads):
                s1, e1, tau = s1_ref[h, js, ls], e1_ref[h, js, ls], tau_ref[h:h + 1, ls]
                for ii in range(gpc):
                    hit = s0_ref[h, 0, ii:ii + 1, ls] + s1 >= tau
                    gates[ii] = gates[ii] + jnp.where(hit, e0_ref[h, 0, ii:ii + 1, ls] * e1, 0.0)
            for ii in range(gpc):
                rows = slice(ii * nk + jb * jh, ii * nk + (jb + 1) * jh)
                h_ref[rows, ls] = (gates[ii] * _gelu_tanh(act_ref[rows, ls])).astype(BF16)


def _peer_main_kernel(xt_ref, u_ref, vt_ref, s0a_ref, e0a_ref, s0b_ref, e0b_ref, s1_ref, e1_ref, tau_ref,
                      x_ref, lg_ref, lb_ref, o_ref, acc_ref, act_a, act_b, h_a, h_b,
                      *, heads, nk, ns, t, alpha):
    g = pl.program_id(1)
    ec = act_a.shape[0]
    mix = functools.partial(_peer_mix, s1_ref=s1_ref, e1_ref=e1_ref, tau_ref=tau_ref, heads=heads, nk=nk, t=t)

    @pl.when(g == 0)
    def _():
        acc_ref[...] = jnp.zeros_like(acc_ref)
        act_b[...] = jnp.zeros_like(act_b)
        h_a[...] = jnp.zeros_like(h_a)
        h_b[...] = jnp.zeros_like(h_b)

    xt = xt_ref[...]
    out_a = _dot(vt_ref[:, :ec], h_a[...])
    mix(s0b_ref, e0b_ref, act_ref=act_b, h_ref=h_b)
    act_a[...] = _dot(u_ref[:ec, :], xt)
    out_b = _dot(vt_ref[:, ec:], h_b[...])
    mix(s0a_ref, e0a_ref, act_ref=act_a, h_ref=h_a)
    act_b[...] = _dot(u_ref[ec:, :], xt)
    acc_ref[...] += out_a + out_b

    @pl.when(g == ns - 1)
    def _():
        o_ref[...] = _layer_norm(alpha * x_ref[...] + acc_ref[...].T, lg_ref[...], lb_ref[...])


def peer_main(xt, u, vt, s0, e0, s1, e1, tau, x, ln_g, ln_b, alpha):
    d, n = xt.shape
    heads, nchunk, gpc, _ = s0.shape
    nk = s1.shape[1]
    n_exp = u.shape[0]
    ec = gpc * nk
    assert n_exp == nchunk * ec and nchunk % 2 == 0
    t = _tile(n, T_PEER, LANES)
    ns = nchunk // 2 + 1
    last = nchunk // 2 - 1
    first_a = pl.BlockSpec((heads, 1, gpc, t), lambda i, g: (0, jnp.minimum(2 * g, nchunk - 1), 0, i))
    first_b = pl.BlockSpec((heads, 1, gpc, t), lambda i, g: (0, jnp.maximum(2 * g - 1, 0), 0, i))
    once = pl.Buffered(1)
    second = pl.BlockSpec((heads, nk, t), lambda i, g: (0, 0, i), pipeline_mode=once)
    return pl.pallas_call(
        functools.partial(_peer_main_kernel, heads=heads, nk=nk, ns=ns, t=t, alpha=alpha),
        grid=(n // t, ns),
        in_specs=[pl.BlockSpec((d, t), lambda i, g: (0, i)),
                  pl.BlockSpec((2 * ec, d), lambda i, g: (jnp.minimum(g, last), 0)),
                  pl.BlockSpec((d, 2 * ec), lambda i, g: (0, jnp.maximum(g - 1, 0))),
                  first_a, first_a, first_b, first_b, second, second,
                  pl.BlockSpec((heads, t), lambda i, g: (0, i)),
                  pl.BlockSpec((t, d), lambda i, g: (i, 0), pipeline_mode=once),
                  pl.BlockSpec((1, d), lambda i, g: (0, 0)),
                  pl.BlockSpec((1, d), lambda i, g: (0, 0))],
        out_specs=pl.BlockSpec((t, d), lambda i, g: (i, 0)),
        out_shape=jax.ShapeDtypeStruct((n, d), F32),
        scratch_shapes=[pltpu.VMEM((d, t), F32), pltpu.VMEM((ec, t), F32), pltpu.VMEM((ec, t), F32),
                        pltpu.VMEM((ec, t), BF16), pltpu.VMEM((ec, t), BF16)],
        compiler_params=_cp("parallel", "arbitrary"),
        name="peer_main",
    )(xt, u, vt, s0, e0, s0, e0, s1, e1, tau, x, ln_g.reshape(1, d), ln_b.reshape(1, d))


def _diff_lambda(lam_vecs, layer_idx):
    lam_init = 0.8 - 0.6 * math.exp(-0.3 * layer_idx)
    lv = lam_vecs.astype(F32)
    lam = jnp.exp(jnp.sum(lv[0] * lv[1])) - jnp.exp(jnp.sum(lv[2] * lv[3])) + lam_init
    return lam.reshape(1), lam_init


def kernel(x_prompt, x_sample, state_ssd_conv, state_ssd_h, cache_diff_k, cache_diff_v, cache_mem_k, cache_mem_v, mem_prompt, ln_g, ln_b, ssd_w_in, ssd_conv_w, ssd_conv_b, ssd_a_log, ssd_dt_bias, ssd_d, ssd_norm_g, ssd_w_out, diff_w_qkv, diff_lam, diff_subln_g, diff_w_o, mem_w_q, mem_w_kv, mem_w_o, peer_w_q, peer_keys, peer_u, peer_v):
    bp, lp, d = x_prompt.shape
    bs, ls, _ = x_sample.shape
    depth = ln_g.shape[0]
    alpha = (2.0 * depth) ** 0.25
    n_mixers = 2
    ssd_heads = ssd_a_log.shape[1]
    d_inner = ssd_norm_g.shape[1]
    conv_dim = ssd_conv_w.shape[2]
    d_state = state_ssd_h.shape[-1]
    groups = (conv_dim - d_inner) // (2 * d_state)
    hpg = ssd_heads // groups
    hd_ssd = d_inner // ssd_heads
    diff_heads = cache_diff_k.shape[3]
    mem_tokens, mem_heads = cache_mem_k.shape[2], cache_mem_k.shape[3]
    peer_heads, _, n_keys, peer_half = peer_keys.shape[1:]

    streams = [
        dict(x=x_prompt.reshape(bp * lp, d), n_seq=bp, seq=lp),
        dict(x=x_sample.reshape(bs * ls, d), n_seq=bs, seq=ls),
    ]
    outs = [dict(conv=[], h=[], dk=[], dv=[]), dict(conv=[], h=[], dk=[], dv=[])]
    mk_p, mv_p = [], []

    def group_lanes(v):
        v = v.reshape(v.shape[:-1] + (groups, hpg))
        v = jnp.pad(v, [(0, 0)] * (v.ndim - 1) + [(0, LANES - hpg)])
        return v.reshape(v.shape[:-2] + (groups * LANES,))

    for i in range(depth):
        j = i // n_mixers
        g0, b0 = ln_g[i, 0], ln_b[i, 0]
        if i % n_mixers == 0:
            w_in = ssd_w_in[j]
            w_z = w_in[:, :d_inner].astype(BF16)
            w_xbc = w_in[:, d_inner:d_inner + conv_dim].astype(BF16)
            w_dt = group_lanes(w_in[:, d_inner + conv_dim:]).astype(BF16)
            dtb = group_lanes(ssd_dt_bias[j]).reshape(1, -1)
            a_neg = group_lanes(-jnp.exp(ssd_a_log[j].astype(F32))).reshape(1, -1)
            d_e = jnp.repeat(ssd_d[j], hd_ssd).reshape(1, d_inner)
            ng = ssd_norm_g[j].reshape(1, d_inner)
            w_out = ssd_w_out[j].astype(BF16)
            for si, st in enumerate(streams):
                x, n_seq, seq = st["x"], st["n_seq"], st["seq"]
                z = matmul(x, w_z)
                xbc = matmul(x, w_xbc)
                dt_raw = matmul(x, w_dt)
                if si == 0:
                    state8 = jnp.zeros((n_seq, SUBLANES, conv_dim), F32)
                    h0, h0_off = jnp.zeros((n_seq, ssd_heads, hd_ssd, d_state), F32), 0
                else:
                    state8 = jnp.pad(state_ssd_conv[j], ((0, 0), (SUBLANES - (CONV_W - 1), 0), (0, 0)))
                    h0, h0_off = state_ssd_h.reshape((-1,) + state_ssd_h.shape[2:]), j * n_seq
                outs[si]["conv"].append(xbc.reshape(n_seq, seq, conv_dim)[:, seq - (CONV_W - 1):])
                y, h_new = ssd_scan(xbc, state8, ssd_conv_w[j], ssd_conv_b[j].reshape(1, conv_dim), z, dt_raw,
                                    dtb, a_neg, d_e, ng, h0, h0_off, n_seq, seq, d_inner, d_state, groups)
                outs[si]["h"].append(h_new)
                st["x"] = matmul_ln(y, w_out, x, g0, b0, alpha)
        else:
            lam, lam_init = _diff_lambda(diff_lam[j], i)
            w_qkv = diff_w_qkv[j]
            w_q, w_k, w_v = (w_qkv[:, c * d:(c + 1) * d].astype(BF16) for c in range(3))
            w_o = diff_w_o[j].astype(BF16)
            for si, st in enumerate(streams):
                x, n_seq, seq = st["x"], st["n_seq"], st["seq"]
                hd_diff = d // diff_heads // 2
                q = matmul(x, w_q, (BF16,), scale=hd_diff ** -0.5 * math.log2(math.e))
                hw = d // diff_heads
                halves = hw // LANES
                k32, k16 = matmul_rows(x, w_k, range(d // LANES))
                v32, v16 = matmul_rows(x, w_v, [(c % halves) * diff_heads + c // halves for c in range(d // LANES)])
                outs[si]["dk"].append(k32.reshape(n_seq, seq, diff_heads, 2, hw // 2))
                outs[si]["dv"].append(v32.reshape(n_seq, seq, halves, diff_heads, LANES)
                                      .transpose(0, 1, 3, 2, 4).reshape(n_seq, seq, diff_heads, hw))
                if si == 0:
                    o = diff_attn_prompt(q, k16, v16, lam, diff_subln_g[j], n_seq, seq, diff_heads, lam_init)
                else:
                    past = cache_diff_k.shape[2]
                    kc = cache_diff_k.reshape(-1, d // LANES, LANES)
                    vc = (cache_diff_v.reshape(-1, diff_heads, halves, LANES).transpose(0, 2, 1, 3)
                          .reshape(-1, d // LANES, LANES))
                    o = diff_attn_sample(q, k16, v16, kc, vc, j * n_seq, past, lam, diff_subln_g[j],
                                         n_seq, seq, diff_heads, lam_init)
                st["x"] = matmul_ln(o, w_o, x, g0, b0, alpha)

        w_mq = mem_w_q[i].astype(BF16)
        w_mk = mem_w_kv[i][:, :d].astype(BF16)
        w_mv = mem_w_kv[i][:, d:].astype(BF16)
        w_mo = mem_w_o[i].astype(BF16)
        mem_rows = mem_prompt.reshape(bp * mem_tokens, d)
        mk = matmul(mem_rows, w_mk)
        mv = matmul(mem_rows, w_mv)
        mk_p.append(mk.reshape(bp, mem_tokens, mem_heads, d // mem_heads))
        mv_p.append(mv.reshape(bp, mem_tokens, mem_heads, d // mem_heads))
        wq_t = peer_w_q[i].T.astype(BF16)
        keys = peer_keys[i].reshape(peer_heads * 2, n_keys, peer_half).astype(BF16)
        u = peer_u[i].astype(BF16)
        vt = peer_v[i].T.astype(BF16)
        for si, st in enumerate(streams):
            x, n_seq, seq = st["x"], st["n_seq"], st["seq"]
            q = matmul(x, w_mq, (BF16,))
            if si == 0:
                kk, vv, kv_off = mk, mv, 0
            else:
                kk, vv, kv_off = cache_mem_k.reshape(-1, d), cache_mem_v.reshape(-1, d), i * n_seq
            o = mem_attn(q, kk, vv, kv_off, mem_tokens, n_seq, seq, mem_heads)
            x, xt = matmul_ln(o, w_mo, x, ln_g[i, 1], ln_b[i, 1], alpha, with_t=True)
            s0, e0, s1, e1, tau = peer_prep(xt, wq_t, keys, EC_PEER // n_keys)
            st["x"] = peer_main(xt, u, vt, s0, e0, s1, e1, tau, x, ln_g[i, 2], ln_b[i, 2], alpha)

    yp = streams[0]["x"].reshape(bp, lp, d)
    ys = streams[1]["x"].reshape(bs, ls, d)
    o_p, o_s = outs
    return (yp, ys, jnp.stack(o_p["conv"]), jnp.stack(o_p["h"]), jnp.stack(o_p["dk"]), jnp.stack(o_p["dv"]),
            jnp.stack(mk_p), jnp.stack(mv_p), jnp.stack(o_s["conv"]), jnp.stack(o_s["h"]),
            jnp.stack(o_s["dk"]), jnp.stack(o_s["dv"]))
```

```python
import functools
import math

import jax
import jax.numpy as jnp
from jax import lax
from jax.experimental import pallas as pl
from jax.experimental.pallas import tpu as pltpu

F32 = jnp.float32
BF16 = jnp.bfloat16

LANES = 128
SUBLANES = 8
VMEM_LIMIT_BYTES = 56 * 1024 * 1024

MASK_CHUNK = 64
CONV_W = 4
PEER_TOPK = 16
LN_EPS = 1e-5
RMS_EPS = 1e-5
SSD_CHUNK = 128
SSD_GROUPS_PER_STEP = 2

TM = 1024
TN = 512
TM_LN = 512
TK = 2048
TQ_DIFF = 1024
TK_DIFF = 512
TQ_MEM = 512
T_PEER = 512
EC_PEER = 512
PEER_J_ROWS = 32

_NT = (((1,), (1,)), ((), ()))


def _tile(n, target, mult):
    t = (min(target, n) // mult) * mult
    while t >= mult:
        if n % t == 0:
            return t
        t -= mult
    return n


def _cp(*sem):
    return pltpu.CompilerParams(dimension_semantics=sem, vmem_limit_bytes=VMEM_LIMIT_BYTES)


def _dot(a, b):
    return jnp.dot(a, b, preferred_element_type=F32)


def _dot_nt(a, b):
    return lax.dot_general(a, b, _NT, preferred_element_type=F32)


def _layer_norm(h, g, b):
    mu = jnp.mean(h, axis=-1, keepdims=True)
    d = h - mu
    var = jnp.mean(d * d, axis=-1, keepdims=True)
    return d * lax.rsqrt(var + LN_EPS) * g + b


def _mm_kernel(a_ref, w_ref, *o_refs, scale):
    acc = _dot(a_ref[...].astype(BF16), w_ref[...])
    if scale is not None:
        acc = acc * scale
    for o in o_refs:
        o[...] = acc.astype(o.dtype)


def matmul(a, w, out_dtypes=(F32,), scale=None):
    n, k = a.shape
    m = w.shape[1]
    tm, tn = _tile(n, TM, 16), _tile(m, TN, LANES)
    outs = pl.pallas_call(
        functools.partial(_mm_kernel, scale=scale),
        grid=(n // tm, m // tn),
        in_specs=[pl.BlockSpec((tm, k), lambda i, j: (i, 0)),
                  pl.BlockSpec((k, tn), lambda i, j: (0, j))],
        out_specs=[pl.BlockSpec((tm, tn), lambda i, j: (i, j)) for _ in out_dtypes],
        out_shape=[jax.ShapeDtypeStruct((n, m), d) for d in out_dtypes],
        compiler_params=_cp("parallel", "parallel"),
        name="matmul",
    )(a, w)
    return outs[0] if len(out_dtypes) == 1 else outs


def _mm_rows_kernel(a_ref, w_ref, lin_ref, o16_ref, *, tm, order):
    acc = _dot(a_ref[...].astype(BF16), w_ref[...])
    o16_ref[...] = acc.astype(BF16)
    nchunk = len(order)
    for c, r in enumerate(order):
        lin_ref[pl.ds(r, tm, stride=nchunk), :] = acc[:, c * LANES:(c + 1) * LANES]


def matmul_rows(a, w, order):
    n, k = a.shape
    m = w.shape[1]
    nchunk = m // LANES
    tm = _tile(n, TM_LN, 16)
    lin, o16 = pl.pallas_call(
        functools.partial(_mm_rows_kernel, tm=tm, order=tuple(order)),
        grid=(n // tm,),
        in_specs=[pl.BlockSpec((tm, k), lambda i: (i, 0)),
                  pl.BlockSpec((k, m), lambda i: (0, 0))],
        out_specs=[pl.BlockSpec((tm * nchunk, LANES), lambda i: (i, 0)),
                   pl.BlockSpec((tm, m), lambda i: (i, 0))],
        out_shape=[jax.ShapeDtypeStruct((n * nchunk, LANES), F32), jax.ShapeDtypeStruct((n, m), BF16)],
        compiler_params=_cp("parallel"),
        name="matmul_rows",
    )(a, w)
    return lin, o16


def _mm_ln_kernel(a_ref, w_ref, x_ref, g_ref, b_ref, *refs, alpha, nk, with_t):
    o_ref = refs[0]
    acc_ref = refs[-1] if nk > 1 else None
    k = pl.program_id(1)
    part = _dot(a_ref[...].astype(BF16), w_ref[...])

    def finish(f):
        y = _layer_norm(alpha * x_ref[...] + f, g_ref[...], b_ref[...])
        o_ref[...] = y
        if with_t:
            refs[1][...] = y.T.astype(BF16)

    if nk == 1:
        finish(part)
        return

    @pl.when(k == 0)
    def _():
        acc_ref[...] = part

    @pl.when(jnp.logical_and(k > 0, k < nk - 1))
    def _():
        acc_ref[...] += part

    @pl.when(k == nk - 1)
    def _():
        finish(acc_ref[...] + part)


def matmul_ln(a, w, x, g, b, alpha, with_t=False):
    n, kdim = a.shape
    d = w.shape[1]
    tm, tk = _tile(n, TM_LN, LANES if with_t else 16), _tile(kdim, TK, LANES)
    nk = kdim // tk
    out_shape = [jax.ShapeDtypeStruct((n, d), F32)]
    out_specs = [pl.BlockSpec((tm, d), lambda i, k: (i, 0))]
    if with_t:
        out_shape.append(jax.ShapeDtypeStruct((d, n), BF16))
        out_specs.append(pl.BlockSpec((d, tm), lambda i, k: (0, i)))
    outs = pl.pallas_call(
        functools.partial(_mm_ln_kernel, alpha=alpha, nk=nk, with_t=with_t),
        grid=(n // tm, nk),
        in_specs=[pl.BlockSpec((tm, tk), lambda i, k: (i, k)),
                  pl.BlockSpec((tk, d), lambda i, k: (k, 0)),
                  pl.BlockSpec((tm, d), lambda i, k: (i, 0)),
                  pl.BlockSpec((1, d), lambda i, k: (0, 0)),
                  pl.BlockSpec((1, d), lambda i, k: (0, 0))],
        out_specs=out_specs,
        out_shape=out_shape,
        scratch_shapes=[pltpu.VMEM((tm, d), F32)] if nk > 1 else [],
        compiler_params=_cp("parallel", "arbitrary"),
        name="matmul_ln",
    )(a, w, x, g.reshape(1, d), b.reshape(1, d))
    return outs if with_t else outs[0]


def _split_bf16(v, parts):
    out, r = [], v
    for _ in range(parts - 1):
        p = r.astype(BF16)
        out.append(p)
        r = r - p.astype(F32)
    out.append(r.astype(BF16))
    return out


def _expand(vals, width, hpg, parts):
    rows = vals[0].shape[0]
    lhs = jnp.concatenate([p for v in vals for p in _split_bf16(v, parts)], axis=0)
    k = lax.broadcasted_iota(jnp.int32, (LANES, hpg * width), 0)
    c = lax.broadcasted_iota(jnp.int32, (LANES, hpg * width), 1) // width
    res = _dot(lhs, jnp.where(k == c, 1.0, 0.0).astype(BF16))
    out = []
    for i in range(len(vals)):
        acc = res[i * parts * rows:(i * parts + 1) * rows]
        for j in range(1, parts):
            acc = acc + res[(i * parts + j) * rows:(i * parts + j + 1) * rows]
        out.append(acc)
    return out


def _ssd_kernel(x_ref, b_ref, c_ref, sx_ref, sb_ref, sc_ref, wx_ref, wb_ref, wc_ref, bx_ref, bb_ref, bc_ref,
                z_ref, dt_ref, dtb_ref, a_ref, d_ref, ng_ref, h0_ref,
                y_ref, hout_ref, ht_ref, kx_ref, kb_ref, kc_ref, *, rows, valid, nc, hpg, hd, gpb):
    c = pl.program_id(2)
    gw = hpg * hd
    ds = b_ref.shape[1] // gpb

    def pad(v):
        if valid == rows:
            return v
        return jnp.concatenate([v, jnp.zeros((rows - valid, v.shape[1]), v.dtype)], axis=0)

    def conv_silu(raw_ref, st_ref, w_ref, bias_ref, keep_ref):
        @pl.when(c == 0)
        def _():
            keep_ref[...] = st_ref[0]

        raw = raw_ref[...]
        ext = jnp.concatenate([keep_ref[...], raw], axis=0)
        w = w_ref[...]
        acc = bias_ref[...] + w[0:1] * ext[5:5 + valid]
        acc = acc + w[1:2] * ext[6:6 + valid]
        acc = acc + w[2:3] * ext[7:7 + valid]
        acc = acc + w[3:4] * raw
        keep_ref[...] = raw[valid - SUBLANES:]
        return pad(acc * jax.nn.sigmoid(acc))

    x_all = conv_silu(x_ref, sx_ref, wx_ref, bx_ref, kx_ref)
    b_all = conv_silu(b_ref, sb_ref, wb_ref, bb_ref, kb_ref)
    c_all = conv_silu(c_ref, sc_ref, wc_ref, bc_ref, kc_ref)

    ri = lax.broadcasted_iota(jnp.int32, (rows, rows), 0)
    ci = lax.broadcasted_iota(jnp.int32, (rows, rows), 1)
    tri = ri >= ci
    trif = tri.astype(F32)
    for gi in range(gpb):
        gs = slice(gi * gw, (gi + 1) * gw)
        ls = slice(gi * LANES, (gi + 1) * LANES)
        ss = slice(gi * ds, (gi + 1) * ds)

        @pl.when(c == 0)
        def _():
            ht_ref[gi] = h0_ref[0, gi * hpg:(gi + 1) * hpg].reshape(gw, ds).T

        x = x_all[:, gs]
        bm = b_all[:, ss]
        cm = c_all[:, ss]
        dt = pad(jax.nn.softplus(dt_ref[:, ls] + dtb_ref[:, ls]))
        da = dt * a_ref[:, ls]
        cum = jnp.dot(trif, da, precision=lax.Precision.HIGHEST,
                      preferred_element_type=F32)
        cum_t = cum.T
        ecum = jnp.exp(cum)
        toend = jnp.exp(cum[rows - 1:rows, :] - cum)
        dt_e, ecum_e, w_e = _expand([dt, ecum, dt * toend], hd, hpg, 2)
        (cum_e,) = _expand([cum], rows, hpg, 3)
        cb16 = cm.astype(BF16)
        cb = _dot_nt(cb16, bm.astype(BF16))
        ht = ht_ref[gi]
        yoff = _dot(cb16, ht.astype(BF16))
        xdt16 = (x * dt_e).astype(BF16)
        ys = []
        for h in range(hpg):
            dec = jnp.exp(jnp.where(tri, cum_e[:, h * rows:(h + 1) * rows] - cum_t[h:h + 1, :], -jnp.inf))
            ys.append(_dot((cb * dec).astype(BF16), xdt16[:, h * hd:(h + 1) * hd]))
        y = jnp.concatenate(ys, axis=1) + yoff * ecum_e
        ht_new = ht * ecum_e[rows - 1:rows, :] + _dot(bm.T.astype(BF16), (x * w_e).astype(BF16))
        ht_ref[gi] = ht_new

        z = pad(z_ref[:, gs])
        y = (y + x * d_ref[:, gs]) * (z * jax.nn.sigmoid(z))
        ms = jnp.mean(y * y, axis=-1, keepdims=True)
        yn = y * lax.rsqrt(ms + RMS_EPS) * ng_ref[:, gs]
        y_ref[:, gs] = yn[:valid].astype(y_ref.dtype)

        @pl.when(c == nc - 1)
        def _():
            hout_ref[0, gi * hpg:(gi + 1) * hpg] = ht_new.T.reshape((hpg,) + hout_ref.shape[2:])


def ssd_scan(xbc, state8, conv_w, conv_b, z, dt_raw, dtb, a, d_e, ng, h0, h0_off, n_seq, seq,
             d_inner, d_state, groups):
    n = xbc.shape[0]
    heads, hd = h0.shape[1], h0.shape[2]
    hpg = heads // groups
    gw = d_inner // groups
    gpb = SSD_GROUPS_PER_STEP
    assert gw == hpg * hd and d_state == LANES and gw % LANES == 0 and groups % gpb == 0
    rows = SSD_CHUNK
    valid = rows if seq % rows == 0 else seq
    assert valid == rows or seq < rows
    nc = seq // valid
    boff, coff = d_inner // (gpb * d_state), (d_inner // d_state + groups) // gpb
    row = lambda s, g, c: s * nc + c
    xw, bw = gpb * gw, gpb * d_state
    cmaps = ((xw, lambda g: g), (bw, lambda g: boff + g), (bw, lambda g: coff + g))

    def per_col(lead):
        return [pl.BlockSpec(tuple(lead) + (w,), lambda s, g, c, cm=cm, k=len(lead): (0,) * k + (cm(g),))
                for w, cm in cmaps]

    y, h_new = pl.pallas_call(
        functools.partial(_ssd_kernel, rows=rows, valid=valid, nc=nc, hpg=hpg, hd=hd, gpb=gpb),
        grid=(n_seq, groups // gpb, nc),
        in_specs=[pl.BlockSpec((valid, w), lambda s, g, c, cm=cm: (row(s, g, c), cm(g))) for w, cm in cmaps]
        + [pl.BlockSpec((1, SUBLANES, w), lambda s, g, c, cm=cm: (s, 0, cm(g))) for w, cm in cmaps]
        + per_col((CONV_W,)) + per_col((1,))
        + [pl.BlockSpec((valid, gpb * gw), lambda s, g, c: (row(s, g, c), g)),
                  pl.BlockSpec((valid, gpb * LANES), lambda s, g, c: (row(s, g, c), g)),
                  pl.BlockSpec((1, gpb * LANES), lambda s, g, c: (0, g)),
                  pl.BlockSpec((1, gpb * LANES), lambda s, g, c: (0, g)),
                  pl.BlockSpec((1, gpb * gw), lambda s, g, c: (0, g)),
                  pl.BlockSpec((1, gpb * gw), lambda s, g, c: (0, g)),
                  pl.BlockSpec((1, gpb * hpg, hd, d_state), lambda s, g, c: (h0_off + s, g, 0, 0))],
        out_specs=[pl.BlockSpec((valid, gpb * gw), lambda s, g, c: (row(s, g, c), g)),
                   pl.BlockSpec((1, gpb * hpg, hd, d_state), lambda s, g, c: (s, g, 0, 0))],
        out_shape=[jax.ShapeDtypeStruct((n, d_inner), BF16),
                   jax.ShapeDtypeStruct((n_seq,) + h0.shape[1:], F32)],
        scratch_shapes=[pltpu.VMEM((gpb, d_state, gw), F32), pltpu.VMEM((SUBLANES, xw), F32),
                        pltpu.VMEM((SUBLANES, bw), F32), pltpu.VMEM((SUBLANES, bw), F32)],
        compiler_params=_cp("parallel", "parallel", "arbitrary"),
        name="ssd_scan",
    )(xbc, xbc, xbc, state8, state8, state8, conv_w, conv_w, conv_w, conv_b, conv_b, conv_b,
      z, dt_raw, dtb, a, d_e, ng, h0)
    return y, h_new


def _lanes(v, w):
    return jnp.concatenate([v] * (w // LANES), axis=1) if w >= LANES else v[:, :w]


def _softmax_step(s, m_ref, l_ref, acc_ref, vb):
    m_old = m_ref[...]
    m_new = jnp.maximum(m_old, jnp.max(s, axis=-1, keepdims=True))
    p = jnp.exp2(s - _lanes(m_new, s.shape[1]))
    corr = jnp.exp2(m_old - m_new)
    l_ref[...] = corr * l_ref[...] + jnp.sum(p, axis=-1, keepdims=True)
    acc_ref[...] = _lanes(corr, acc_ref.shape[1]) * acc_ref[...] + _dot(p.astype(BF16), vb)
    m_ref[...] = m_new


def _diff_finish(lam, lam_init, g_ref, o_ref, l0, a0, l1, a1):
    w = a0.shape[1]
    o = a0[...] / _lanes(l0[...], w) - lam * (a1[...] / _lanes(l1[...], w))
    ms = jnp.mean(o * o, axis=-1, keepdims=True)
    o_ref[...] = (o * lax.rsqrt(ms + RMS_EPS) * g_ref[...] * (1.0 - lam_init)).astype(o_ref.dtype)


def _diff_init(m0, l0, a0, m1, l1, a1):
    for m, l, a in ((m0, l0, a0), (m1, l1, a1)):
        m[...] = jnp.full_like(m, -jnp.inf)
        l[...] = jnp.zeros_like(l)
        a[...] = jnp.zeros_like(a)


def _diff_prompt_kernel(lam_ref, q_ref, k_ref, v_ref, g_ref, o_ref, m0, l0, a0, m1, l1, a1,
                        *, tq, tk, hd, lam_init):
    qi = pl.program_id(2)
    _diff_init(m0, l0, a0, m1, l1, a1)
    q = q_ref[...]
    q0, q1 = q[:, :hd], q[:, hd:]
    per_q = tq // tk

    def block(ki, mask, r0=0):
        off = pl.multiple_of(ki * tk, tk)
        kb = k_ref[pl.ds(off, tk), :]
        vb = v_ref[pl.ds(off, tk), :]
        rows = pl.ds(r0, tq - r0)
        for qj, kj, m, l, a in ((q0, kb[:, :hd], m0, l0, a0), (q1, kb[:, hd:], m1, l1, a1)):
            s = _dot_nt(qj[r0:], kj)
            if mask is not None:
                s = jnp.where(mask, s, -jnp.inf)
            _softmax_step(s, m.at[rows], l.at[rows], a.at[rows], vb)

    def body(ki, carry):
        block(ki, None)
        return carry

    lax.fori_loop(0, qi * per_q, body, 0)
    for d in range(per_q):
        r0 = d * tk
        rc = (lax.broadcasted_iota(jnp.int32, (tq - r0, tk), 0) + r0) // MASK_CHUNK
        cc = (lax.broadcasted_iota(jnp.int32, (tq - r0, tk), 1) + r0) // MASK_CHUNK
        block(qi * per_q + d, cc <= rc, r0)
    _diff_finish(lam_ref[0], lam_init, g_ref, o_ref, l0, a0, l1, a1)


def diff_attn_prompt(q, k, v, lam, g, n_seq, seq, heads, lam_init):
    n, d = q.shape
    hw = d // heads
    hd = hw // 2
    tq = _tile(seq, TQ_DIFF, MASK_CHUNK)
    tk = _tile(tq, TK_DIFF, MASK_CHUNK)
    qb = seq // tq
    stat = pltpu.VMEM((tq, LANES), F32)
    acc = pltpu.VMEM((tq, hw), F32)
    return pl.pallas_call(
        functools.partial(_diff_prompt_kernel, tq=tq, tk=tk, hd=hd, lam_init=lam_init),
        grid=(n_seq, heads, qb),
        in_specs=[pl.BlockSpec(memory_space=pltpu.SMEM),
                  pl.BlockSpec((tq, hw), lambda b, h, i: (b * qb + i, h)),
                  pl.BlockSpec((seq, hw), lambda b, h, i: (b, h)),
                  pl.BlockSpec((seq, hw), lambda b, h, i: (b, h)),
                  pl.BlockSpec((1, hw), lambda b, h, i: (0, 0))],
        out_specs=pl.BlockSpec((tq, hw), lambda b, h, i: (b * qb + i, h)),
        out_shape=jax.ShapeDtypeStruct((n, d), BF16),
        scratch_shapes=[stat, stat, acc, stat, stat, acc],
        compiler_params=_cp("parallel", "parallel", "arbitrary"),
        name="diff_attn_prompt",
    )(lam, q, k, v, g.reshape(1, hw))


def _diff_sample_kernel(lam_ref, q_ref, kc_ref, vc_ref, kn_ref, vn_ref, g_ref, o_ref, m_ref, l_ref, a_ref,
                        *, heads, nkc, hd, lam_init):
    c = pl.program_id(1)
    hw = 2 * hd
    halves = hw // LANES

    @pl.when(c == 0)
    def _():
        m_ref[...] = jnp.full_like(m_ref, -jnp.inf)
        l_ref[...] = jnp.zeros_like(l_ref)
        a_ref[...] = jnp.zeros_like(a_ref)

    def step(h, j, kb, vb):
        i = 2 * h + j
        qj = q_ref[:, i * hd:(i + 1) * hd]
        _softmax_step(_dot_nt(qj, kb), m_ref.at[i], l_ref.at[i], a_ref.at[i], vb)

    @pl.when(c < nkc)
    def _():
        for h in range(heads):
            vb = jnp.concatenate([vc_ref[:, e * heads + h, :] for e in range(halves)], axis=1).astype(BF16)
            for j in (0, 1):
                step(h, j, kc_ref[:, 2 * h + j, :].astype(BF16), vb)

    @pl.when(c == nkc)
    def _():
        for h in range(heads):
            vb = vn_ref[:, h * hw:(h + 1) * hw]
            for j in (0, 1):
                step(h, j, kn_ref[:, (2 * h + j) * hd:(2 * h + j + 1) * hd], vb)
            _diff_finish(lam_ref[0], lam_init, g_ref, o_ref.at[:, h * hw:(h + 1) * hw],
                         l_ref.at[2 * h], a_ref.at[2 * h], l_ref.at[2 * h + 1], a_ref.at[2 * h + 1])


def diff_attn_sample(q, k_new, v_new, k_cache, v_cache, cache_off, past, lam, g, n_seq, seq, heads, lam_init):
    n, d = q.shape
    hw = d // heads
    hd = hw // 2
    nch = d // LANES
    tk = _tile(past, TK_DIFF, SUBLANES)
    nkc = past // tk
    rows = pl.BlockSpec((seq, d), lambda b, c: (b, 0))
    cache = pl.BlockSpec((tk, nch, LANES), lambda b, c: ((cache_off + b) * nkc + jnp.minimum(c, nkc - 1), 0, 0))
    return pl.pallas_call(
        functools.partial(_diff_sample_kernel, heads=heads, nkc=nkc, hd=hd, lam_init=lam_init),
        grid=(n_seq, nkc + 1),
        in_specs=[pl.BlockSpec(memory_space=pltpu.SMEM), rows, cache, cache, rows, rows,
                  pl.BlockSpec((1, hw), lambda b, c: (0, 0))],
        out_specs=rows,
        out_shape=jax.ShapeDtypeStruct((n, d), BF16),
        scratch_shapes=[pltpu.VMEM((2 * heads, seq, LANES), F32), pltpu.VMEM((2 * heads, seq, LANES), F32),
                        pltpu.VMEM((2 * heads, seq, hw), F32)],
        compiler_params=_cp("parallel", "arbitrary"),
        name="diff_attn_sample",
    )(lam, q, k_cache, v_cache, k_new, v_new, g.reshape(1, hw))


def _mem_attn_kernel(q_ref, k_ref, v_ref, o_ref, *, heads, hd, scale):
    for h in range(heads):
        sl = slice(h * hd, (h + 1) * hd)
        s = _dot_nt(q_ref[:, sl], k_ref[:, sl].astype(BF16)) * scale
        p = jnp.exp(s - jnp.max(s, axis=-1, keepdims=True))
        l = jnp.sum(p, axis=-1, keepdims=True)
        o = _dot(p.astype(BF16), v_ref[:, sl].astype(BF16)) / l
        o_ref[:, sl] = o.astype(o_ref.dtype)


def mem_attn(q, mk, mv, kv_off, m, n_seq, seq, heads):
    n, d = q.shape
    hd = d // heads
    tq = _tile(seq, TQ_MEM, 16)
    qb = seq // tq
    kv = pl.BlockSpec((m, d), lambda b, i: (kv_off + b, 0))
    return pl.pallas_call(
        functools.partial(_mem_attn_kernel, heads=heads, hd=hd, scale=hd ** -0.5),
        grid=(n_seq, qb),
        in_specs=[pl.BlockSpec((tq, d), lambda b, i: (b * qb + i, 0)), kv, kv],
        out_specs=pl.BlockSpec((tq, d), lambda b, i: (b * qb + i, 0)),
        out_shape=jax.ShapeDtypeStruct((n, d), BF16),
        compiler_params=_cp("parallel", "parallel"),
        name="mem_attn",
    )(q, mk, mv)


def _top_rows(v, k):
    rows = []
    for _ in range(k):
        m = jnp.max(v, axis=0, keepdims=True)
        rows.append(m)
        v = jnp.where(v == m, -jnp.inf, v)
    return rows


def _sort_network(lo, hi):
    def merge(lo, hi, r):
        step = r * 2
        if step < hi - lo:
            yield from merge(lo, hi, step)
            yield from merge(lo + r, hi, step)
            yield from [(i, i + r) for i in range(lo + r, hi - r, step)]
        else:
            yield (lo, lo + r)

    if hi - lo >= 1:
        mid = lo + (hi - lo) // 2
        yield from _sort_network(lo, mid)
        yield from _sort_network(mid + 1, hi)
        yield from merge(lo, hi, 1)


def _top_sorted(v):
    n = v.shape[0] // SUBLANES
    x = [v[r * SUBLANES:(r + 1) * SUBLANES] for r in range(n)]

    def order(i, j):
        x[i], x[j] = jnp.maximum(x[i], x[j]), jnp.minimum(x[i], x[j])

    for i, j in _sort_network(0, n - 1):
        order(i, j)
    shift = SUBLANES // 2
    while shift >= 1:
        x = [jnp.maximum(x[r], pltpu.roll(x[n - 1 - r], shift, 0)) for r in range(n)]
        d = n // 2
        while d >= 1:
            for i in range(n):
                if i & d == 0:
                    order(i, i + d)
            d //= 2
        shift //= 2
    return [xr[0:1] for xr in x]


def _peer_prep_kernel(xt_ref, wq_ref, keys_ref, s0_ref, e0_ref, s1_ref, e1_ref, tau_ref, *, heads, gpc):
    k = PEER_TOPK
    qt = _dot(wq_ref[...], xt_ref[...]).astype(BF16)
    half = keys_ref.shape[2]
    for h in range(heads):
        sc = [_dot(keys_ref[2 * h + j], qt[(2 * h + j) * half:(2 * h + j + 1) * half]) for j in (0, 1)]
        a = _top_sorted(sc[0])
        b = jnp.concatenate(_top_sorted(sc[1]), axis=0)
        cand = [a[0] + b] + [a[r] + b[:k // 2] for r in range(1, k // 2)]
        cand.append(jnp.concatenate(a[k // 2:], axis=0) + b[0:1])
        cand = jnp.concatenate(cand, axis=0)
        tau = _top_rows(cand, k)[-1]
        mx = a[0] + b[0:1]
        z = jnp.sum(jnp.where(cand >= tau, jnp.exp(cand - mx), 0.0), axis=0, keepdims=True)
        e0 = jnp.exp(sc[0] - a[0])
        for c in range(s0_ref.shape[1]):
            s0_ref[h, c] = sc[0][c * gpc:(c + 1) * gpc]
            e0_ref[h, c] = e0[c * gpc:(c + 1) * gpc]
        s1_ref[h] = sc[1]
        e1_ref[h] = jnp.exp(sc[1] - b[0:1]) / z
        tau_ref[h:h + 1, :] = tau


def peer_prep(xt, wq_t, keys, gpc):
    d, n = xt.shape
    hj, nk, half = keys.shape
    heads = hj // 2
    assert PEER_TOPK % 2 == 0 and nk % gpc == 0 and nk == SUBLANES * PEER_TOPK
    t = _tile(n, T_PEER, LANES)
    first = pl.BlockSpec((heads, nk // gpc, gpc, t), lambda i: (0, 0, 0, i))
    second = pl.BlockSpec((heads, nk, t), lambda i: (0, 0, i))
    f_shape = jax.ShapeDtypeStruct((heads, nk // gpc, gpc, n), F32)
    s_shape = jax.ShapeDtypeStruct((heads, nk, n), F32)
    return pl.pallas_call(
        functools.partial(_peer_prep_kernel, heads=heads, gpc=gpc),
        grid=(n // t,),
        in_specs=[pl.BlockSpec((d, t), lambda i: (0, i)),
                  pl.BlockSpec(wq_t.shape, lambda i: (0, 0)),
                  pl.BlockSpec(keys.shape, lambda i: (0, 0, 0))],
        out_specs=[first, first, second, second, pl.BlockSpec((heads, t), lambda i: (0, i))],
        out_shape=[f_shape, f_shape, s_shape, s_shape, jax.ShapeDtypeStruct((heads, n), F32)],
        compiler_params=_cp("parallel"),
        name="peer_prep",
    )(xt, wq_t, keys)


def _gelu_tanh(x):
    c = math.sqrt(2.0 / math.pi)
    half = 0.5 * x
    return half + half * jnp.tanh(x * (c + (c * 0.044715) * (x * x)))


def _peer_mix(s0_ref, e0_ref, s1_ref, e1_ref, tau_ref, act_ref, h_ref, *, heads, nk, t):
    gpc = act_ref.shape[0] // nk
    jh = PEER_J_ROWS
    for tl in range(t // LANES):
        ls = slice(tl * LANES, (tl + 1) * LANES)
        for jb in range(nk // jh):
            js = slice(jb * jh, (jb + 1) * jh)
            gates = [jnp.zeros((jh, LANES), F32) for _ in range(gpc)]
            for h in range(heads):
                s1, e1, tau = s1_ref[h, js, ls], e1_ref[h, js, ls], tau_ref[h:h + 1, ls]
                for ii in range(gpc):
                    hit = s0_ref[h, 0, ii:ii + 1, ls] + s1 >= tau
                    gates[ii] = gates[ii] + jnp.where(hit, e0_ref[h, 0, ii:ii + 1, ls] * e1, 0.0)
            for ii in range(gpc):
                rows = slice(ii * nk + jb * jh, ii * nk + (jb + 1) * jh)
                h_ref[rows, ls] = (gates[ii] * _gelu_tanh(act_ref[rows, ls])).astype(BF16)


def _peer_main_kernel(xt_ref, u_ref, vt_ref, s0a_ref, e0a_ref, s0b_ref, e0b_ref, s1_ref, e1_ref, tau_ref,
                      x_ref, lg_ref, lb_ref, o_ref, acc_ref, act_a, act_b, h_a, h_b,
                      *, heads, nk, ns, t, alpha):
    g = pl.program_id(1)
    ec = act_a.shape[0]
    mix = functools.partial(_peer_mix, s1_ref=s1_ref, e1_ref=e1_ref, tau_ref=tau_ref, heads=heads, nk=nk, t=t)

    @pl.when(g == 0)
    def _():
        acc_ref[...] = jnp.zeros_like(acc_ref)
        act_b[...] = jnp.zeros_like(act_b)
        h_a[...] = jnp.zeros_like(h_a)
        h_b[...] = jnp.zeros_like(h_b)

    xt = xt_ref[...]
    out_a = _dot(vt_ref[:, :ec], h_a[...])
    mix(s0b_ref, e0b_ref, act_ref=act_b, h_ref=h_b)
    act_a[...] = _dot(u_ref[:ec, :], xt)
    out_b = _dot(vt_ref[:, ec:], h_b[...])
    mix(s0a_ref, e0a_ref, act_ref=act_a, h_ref=h_a)
    act_b[...] = _dot(u_ref[ec:, :], xt)
    acc_ref[...] += out_a + out_b

    @pl.when(g == ns - 1)
    def _():
        o_ref[...] = _layer_norm(alpha * x_ref[...] + acc_ref[...].T, lg_ref[...], lb_ref[...])


def peer_main(xt, u, vt, s0, e0, s1, e1, tau, x, ln_g, ln_b, alpha):
    d, n = xt.shape
    heads, nchunk, gpc, _ = s0.shape
    nk = s1.shape[1]
    n_exp = u.shape[0]
    ec = gpc * nk
    assert n_exp == nchunk * ec and nchunk % 2 == 0
    t = _tile(n, T_PEER, LANES)
    ns = nchunk // 2 + 1
    last = nchunk // 2 - 1
    first_a = pl.BlockSpec((heads, 1, gpc, t), lambda i, g: (0, jnp.minimum(2 * g, nchunk - 1), 0, i))
    first_b = pl.BlockSpec((heads, 1, gpc, t), lambda i, g: (0, jnp.maximum(2 * g - 1, 0), 0, i))
    once = pl.Buffered(1)
    second = pl.BlockSpec((heads, nk, t), lambda i, g: (0, 0, i), pipeline_mode=once)
    return pl.pallas_call(
        functools.partial(_peer_main_kernel, heads=heads, nk=nk, ns=ns, t=t, alpha=alpha),
        grid=(n // t, ns),
        in_specs=[pl.BlockSpec((d, t), lambda i, g: (0, i)),
                  pl.BlockSpec((2 * ec, d), lambda i, g: (jnp.minimum(g, last), 0)),
                  pl.BlockSpec((d, 2 * ec), lambda i, g: (0, jnp.maximum(g - 1, 0))),
                  first_a, first_a, first_b, first_b, second, second,
                  pl.BlockSpec((heads, t), lambda i, g: (0, i)),
                  pl.BlockSpec((t, d), lambda i, g: (i, 0), pipeline_mode=once),
                  pl.BlockSpec((1, d), lambda i, g: (0, 0)),
                  pl.BlockSpec((1, d), lambda i, g: (0, 0))],
        out_specs=pl.BlockSpec((t, d), lambda i, g: (i, 0)),
        out_shape=jax.ShapeDtypeStruct((n, d), F32),
        scratch_shapes=[pltpu.VMEM((d, t), F32), pltpu.VMEM((ec, t), F32), pltpu.VMEM((ec, t), F32),
                        pltpu.VMEM((ec, t), BF16), pltpu.VMEM((ec, t), BF16)],
        compiler_params=_cp("parallel", "arbitrary"),
        name="peer_main",
    )(xt, u, vt, s0, e0, s0, e0, s1, e1, tau, x, ln_g.reshape(1, d), ln_b.reshape(1, d))


def _diff_lambda(lam_vecs, layer_idx):
    lam_init = 0.8 - 0.6 * math.exp(-0.3 * layer_idx)
    lv = lam_vecs.astype(F32)
    lam = jnp.exp(jnp.sum(lv[0] * lv[1])) - jnp.exp(jnp.sum(lv[2] * lv[3])) + lam_init
    return lam.reshape(1), lam_init


def kernel(x_prompt, x_sample, state_ssd_conv, state_ssd_h, cache_diff_k, cache_diff_v, cache_mem_k, cache_mem_v, mem_prompt, ln_g, ln_b, ssd_w_in, ssd_conv_w, ssd_conv_b, ssd_a_log, ssd_dt_bias, ssd_d, ssd_norm_g, ssd_w_out, diff_w_qkv, diff_lam, diff_subln_g, diff_w_o, mem_w_q, mem_w_kv, mem_w_o, peer_w_q, peer_keys, peer_u, peer_v):
    bp, lp, d = x_prompt.shape
    bs, ls, _ = x_sample.shape
    depth = ln_g.shape[0]
    alpha = (2.0 * depth) ** 0.25
    n_mixers = 2
    ssd_heads = ssd_a_log.shape[1]
    d_inner = ssd_norm_g.shape[1]
    conv_dim = ssd_conv_w.shape[2]
    d_state = state_ssd_h.shape[-1]
    groups = (conv_dim - d_inner) // (2 * d_state)
    hpg = ssd_heads // groups
    hd_ssd = d_inner // ssd_heads
    diff_heads = cache_diff_k.shape[3]
    mem_tokens, mem_heads = cache_mem_k.shape[2], cache_mem_k.shape[3]
    peer_heads, _, n_keys, peer_half = peer_keys.shape[1:]

    streams = [
        dict(x=x_prompt.reshape(bp * lp, d), n_seq=bp, seq=lp),
        dict(x=x_sample.reshape(bs * ls, d), n_seq=bs, seq=ls),
    ]
    outs = [dict(conv=[], h=[], dk=[], dv=[]), dict(conv=[], h=[], dk=[], dv=[])]
    mk_p, mv_p = [], []

    def group_lanes(v):
        v = v.reshape(v.shape[:-1] + (groups, hpg))
        v = jnp.pad(v, [(0, 0)] * (v.ndim - 1) + [(0, LANES - hpg)])
        return v.reshape(v.shape[:-2] + (groups * LANES,))

    for i in range(depth):
        j = i // n_mixers
        g0, b0 = ln_g[i, 0], ln_b[i, 0]
        if i % n_mixers == 0:
            w_in = ssd_w_in[j]
            w_z = w_in[:, :d_inner].astype(BF16)
            w_xbc = w_in[:, d_inner:d_inner + conv_dim].astype(BF16)
            w_dt = group_lanes(w_in[:, d_inner + conv_dim:]).astype(BF16)
            dtb = group_lanes(ssd_dt_bias[j]).reshape(1, -1)
            a_neg = group_lanes(-jnp.exp(ssd_a_log[j].astype(F32))).reshape(1, -1)
            d_e = jnp.repeat(ssd_d[j], hd_ssd).reshape(1, d_inner)
            ng = ssd_norm_g[j].reshape(1, d_inner)
            w_out = ssd_w_out[j].astype(BF16)
            for si, st in enumerate(streams):
                x, n_seq, seq = st["x"], st["n_seq"], st["seq"]
                z = matmul(x, w_z)
                xbc = matmul(x, w_xbc)
                dt_raw = matmul(x, w_dt)
                if si == 0:
                    state8 = jnp.zeros((n_seq, SUBLANES, conv_dim), F32)
                    h0, h0_off = jnp.zeros((n_seq, ssd_heads, hd_ssd, d_state), F32), 0
                else:
                    state8 = jnp.pad(state_ssd_conv[j], ((0, 0), (SUBLANES - (CONV_W - 1), 0), (0, 0)))
                    h0, h0_off = state_ssd_h.reshape((-1,) + state_ssd_h.shape[2:]), j * n_seq
                outs[si]["conv"].append(xbc.reshape(n_seq, seq, conv_dim)[:, seq - (CONV_W - 1):])
                y, h_new = ssd_scan(xbc, state8, ssd_conv_w[j], ssd_conv_b[j].reshape(1, conv_dim), z, dt_raw,
                                    dtb, a_neg, d_e, ng, h0, h0_off, n_seq, seq, d_inner, d_state, groups)
                outs[si]["h"].append(h_new)
                st["x"] = matmul_ln(y, w_out, x, g0, b0, alpha)
        else:
            lam, lam_init = _diff_lambda(diff_lam[j], i)
            w_qkv = diff_w_qkv[j]
            w_q, w_k, w_v = (w_qkv[:, c * d:(c + 1) * d].astype(BF16) for c in range(3))
            w_o = diff_w_o[j].astype(BF16)
            for si, st in enumerate(streams):
                x, n_seq, seq = st["x"], st["n_seq"], st["seq"]
                hd_diff = d // diff_heads // 2
                q = matmul(x, w_q, (BF16,), scale=hd_diff ** -0.5 * math.log2(math.e))
                hw = d // diff_heads
                halves = hw // LANES
                k32, k16 = matmul_rows(x, w_k, range(d // LANES))
                v32, v16 = matmul_rows(x, w_v, [(c % halves) * diff_heads + c // halves for c in range(d // LANES)])
                outs[si]["dk"].append(k32.reshape(n_seq, seq, diff_heads, 2, hw // 2))
                outs[si]["dv"].append(v32.reshape(n_seq, seq, halves, diff_heads, LANES)
                                      .transpose(0, 1, 3, 2, 4).reshape(n_seq, seq, diff_heads, hw))
                if si == 0:
                    o = diff_attn_prompt(q, k16, v16, lam, diff_subln_g[j], n_seq, seq, diff_heads, lam_init)
                else:
                    past = cache_diff_k.shape[2]
                    kc = cache_diff_k.reshape(-1, d // LANES, LANES)
                    vc = (cache_diff_v.reshape(-1, diff_heads, halves, LANES).transpose(0, 2, 1, 3)
                          .reshape(-1, d // LANES, LANES))
                    o = diff_attn_sample(q, k16, v16, kc, vc, j * n_seq, past, lam, diff_subln_g[j],
                                         n_seq, seq, diff_heads, lam_init)
                st["x"] = matmul_ln(o, w_o, x, g0, b0, alpha)

        w_mq = mem_w_q[i].astype(BF16)
        w_mk = mem_w_kv[i][:, :d].astype(BF16)
        w_mv = mem_w_kv[i][:, d:].astype(BF16)
        w_mo = mem_w_o[i].astype(BF16)
        mem_rows = mem_prompt.reshape(bp * mem_tokens, d)
        mk = matmul(mem_rows, w_mk)
        mv = matmul(mem_rows, w_mv)
        mk_p.append(mk.reshape(bp, mem_tokens, mem_heads, d // mem_heads))
        mv_p.append(mv.reshape(bp, mem_tokens, mem_heads, d // mem_heads))
        wq_t = peer_w_q[i].T.astype(BF16)
        keys = peer_keys[i].reshape(peer_heads * 2, n_keys, peer_half).astype(BF16)
        u = peer_u[i].astype(BF16)
        vt = peer_v[i].T.astype(BF16)
        for si, st in enumerate(streams):
            x, n_seq, seq = st["x"], st["n_seq"], st["seq"]
            q = matmul(x, w_mq, (BF16,))
            if si == 0:
                kk, vv, kv_off = mk, mv, 0
            else:
                kk, vv, kv_off = cache_mem_k.reshape(-1, d), cache_mem_v.reshape(-1, d), i * n_seq
            o = mem_attn(q, kk, vv, kv_off, mem_tokens, n_seq, seq, mem_heads)
            x, xt = matmul_ln(o, w_mo, x, ln_g[i, 1], ln_b[i, 1], alpha, with_t=True)
            s0, e0, s1, e1, tau = peer_prep(xt, wq_t, keys, EC_PEER // n_keys)
            st["x"] = peer_main(xt, u, vt, s0, e0, s1, e1, tau, x, ln_g[i, 2], ln_b[i, 2], alpha)

    yp = streams[0]["x"].reshape(bp, lp, d)
    ys = streams[1]["x"].reshape(bs, ls, d)
    o_p, o_s = outs
    return (yp, ys, jnp.stack(o_p["conv"]), jnp.stack(o_p["h"]), jnp.stack(o_p["dk"]), jnp.stack(o_p["dv"]),
            jnp.stack(mk_p), jnp.stack(mv_p), jnp.stack(o_s["conv"]), jnp.stack(o_s["h"]),
            jnp.stack(o_s["dk"]), jnp.stack(o_s["dv"]))
```

```python
import functools
import math

import jax
import jax.numpy as jnp
from jax import lax
from jax.experimental import pallas as pl
from jax.experimental.pallas import tpu as pltpu

F32 = jnp.float32
BF16 = jnp.bfloat16

LANES = 128
SUBLANES = 8
VMEM_LIMIT_BYTES = 56 * 1024 * 1024

MASK_CHUNK = 64
CONV_W = 4
PEER_TOPK = 16
LN_EPS = 1e-5
RMS_EPS = 1e-5
SSD_CHUNK = 128
SSD_GROUPS_PER_STEP = 2

TM = 1024
TN = 512
TM_LN = 512
TK = 2048
TQ_DIFF = 1024
TK_DIFF = 512
TQ_MEM = 512
T_PEER = 512
EC_PEER = 512
PEER_J_ROWS = 32

_NT = (((1,), (1,)), ((), ()))


def _tile(n, target, mult):
    t = (min(target, n) // mult) * mult
    while t >= mult:
        if n % t == 0:
            return t
        t -= mult
    return n


def _cp(*sem):
    return pltpu.CompilerParams(dimension_semantics=sem, vmem_limit_bytes=VMEM_LIMIT_BYTES)


def _dot(a, b):
    return jnp.dot(a, b, preferred_element_type=F32)


def _dot_nt(a, b):
    return lax.dot_general(a, b, _NT, preferred_element_type=F32)


def _layer_norm(h, g, b):
    mu = jnp.mean(h, axis=-1, keepdims=True)
    d = h - mu
    var = jnp.mean(d * d, axis=-1, keepdims=True)
    return d * lax.rsqrt(var + LN_EPS) * g + b


def _mm_kernel(a_ref, w_ref, *o_refs, scale):
    acc = _dot(a_ref[...].astype(BF16), w_ref[...])
    if scale is not None:
        acc = acc * scale
    for o in o_refs:
        o[...] = acc.astype(o.dtype)


def matmul(a, w, out_dtypes=(F32,), scale=None):
    n, k = a.shape
    m = w.shape[1]
    tm, tn = _tile(n, TM, 16), _tile(m, TN, LANES)
    outs = pl.pallas_call(
        functools.partial(_mm_kernel, scale=scale),
        grid=(n // tm, m // tn),
        in_specs=[pl.BlockSpec((tm, k), lambda i, j: (i, 0)),
                  pl.BlockSpec((k, tn), lambda i, j: (0, j))],
        out_specs=[pl.BlockSpec((tm, tn), lambda i, j: (i, j)) for _ in out_dtypes],
        out_shape=[jax.ShapeDtypeStruct((n, m), d) for d in out_dtypes],
        compiler_params=_cp("parallel", "parallel"),
        name="matmul",
    )(a, w)
    return outs[0] if len(out_dtypes) == 1 else outs


def _mm_rows_kernel(a_ref, w_ref, lin_ref, o16_ref, *, tm, order):
    acc = _dot(a_ref[...].astype(BF16), w_ref[...])
    o16_ref[...] = acc.astype(BF16)
    nchunk = len(order)
    for c, r in enumerate(order):
        lin_ref[pl.ds(r, tm, stride=nchunk), :] = acc[:, c * LANES:(c + 1) * LANES]


def matmul_rows(a, w, order):
    n, k = a.shape
    m = w.shape[1]
    nchunk = m // LANES
    tm = _tile(n, TM_LN, 16)
    lin, o16 = pl.pallas_call(
        functools.partial(_mm_rows_kernel, tm=tm, order=tuple(order)),
        grid=(n // tm,),
        in_specs=[pl.BlockSpec((tm, k), lambda i: (i, 0)),
                  pl.BlockSpec((k, m), lambda i: (0, 0))],
        out_specs=[pl.BlockSpec((tm * nchunk, LANES), lambda i: (i, 0)),
                   pl.BlockSpec((tm, m), lambda i: (i, 0))],
        out_shape=[jax.ShapeDtypeStruct((n * nchunk, LANES), F32), jax.ShapeDtypeStruct((n, m), BF16)],
        compiler_params=_cp("parallel"),
        name="matmul_rows",
    )(a, w)
    return lin, o16


def _mm_ln_kernel(a_ref, w_ref, x_ref, g_ref, b_ref, *refs, alpha, nk, with_t):
    o_ref = refs[0]
    acc_ref = refs[-1] if nk > 1 else None
    k = pl.program_id(1)
    part = _dot(a_ref[...].astype(BF16), w_ref[...])

    def finish(f):
        y = _layer_norm(alpha * x_ref[...] + f, g_ref[...], b_ref[...])
        o_ref[...] = y
        if with_t:
            refs[1][...] = y.T.astype(BF16)

    if nk == 1:
        finish(part)
        return

    @pl.when(k == 0)
    def _():
        acc_ref[...] = part

    @pl.when(jnp.logical_and(k > 0, k < nk - 1))
    def _():
        acc_ref[...] += part

    @pl.when(k == nk - 1)
    def _():
        finish(acc_ref[...] + part)


def matmul_ln(a, w, x, g, b, alpha, with_t=False):
    n, kdim = a.shape
    d = w.shape[1]
    tm, tk = _tile(n, TM_LN, LANES if with_t else 16), _tile(kdim, TK, LANES)
    nk = kdim // tk
    out_shape = [jax.ShapeDtypeStruct((n, d), F32)]
    out_specs = [pl.BlockSpec((tm, d), lambda i, k: (i, 0))]
    if with_t:
        out_shape.append(jax.ShapeDtypeStruct((d, n), BF16))
        out_specs.append(pl.BlockSpec((d, tm), lambda i, k: (0, i)))
    outs = pl.pallas_call(
        functools.partial(_mm_ln_kernel, alpha=alpha, nk=nk, with_t=with_t),
        grid=(n // tm, nk),
        in_specs=[pl.BlockSpec((tm, tk), lambda i, k: (i, k)),
                  pl.BlockSpec((tk, d), lambda i, k: (k, 0)),
                  pl.BlockSpec((tm, d), lambda i, k: (i, 0)),
                  pl.BlockSpec((1, d), lambda i, k: (0, 0)),
                  pl.BlockSpec((1, d), lambda i, k: (0, 0))],
        out_specs=out_specs,
        out_shape=out_shape,
        scratch_shapes=[pltpu.VMEM((tm, d), F32)] if nk > 1 else [],
        compiler_params=_cp("parallel", "arbitrary"),
        name="matmul_ln",
    )(a, w, x, g.reshape(1, d), b.reshape(1, d))
    return outs if with_t else outs[0]


def _split_bf16(v, parts):
    out, r = [], v
    for _ in range(parts - 1):
        p = r.astype(BF16)
        out.append(p)
        r = r - p.astype(F32)
    out.append(r.astype(BF16))
    return out


def _expand(vals, width, hpg, parts):
    rows = vals[0].shape[0]
    lhs = jnp.concatenate([p for v in vals for p in _split_bf16(v, parts)], axis=0)
    k = lax.broadcasted_iota(jnp.int32, (LANES, hpg * width), 0)
    c = lax.broadcasted_iota(jnp.int32, (LANES, hpg * width), 1) // width
    res = _dot(lhs, jnp.where(k == c, 1.0, 0.0).astype(BF16))
    out = []
    for i in range(len(vals)):
        acc = res[i * parts * rows:(i * parts + 1) * rows]
        for j in range(1, parts):
            acc = acc + res[(i * parts + j) * rows:(i * parts + j + 1) * rows]
        out.append(acc)
    return out


def _ssd_kernel(x_ref, b_ref, c_ref, sx_ref, sb_ref, sc_ref, wx_ref, wb_ref, wc_ref, bx_ref, bb_ref, bc_ref,
                z_ref, dt_ref, dtb_ref, a_ref, d_ref, ng_ref, h0_ref,
                y_ref, hout_ref, ht_ref, kx_ref, kb_ref, kc_ref, *, rows, valid, nc, hpg, hd, gpb):
    c = pl.program_id(2)
    gw = hpg * hd
    ds = b_ref.shape[1] // gpb

    def pad(v):
        if valid == rows:
            return v
        return jnp.concatenate([v, jnp.zeros((rows - valid, v.shape[1]), v.dtype)], axis=0)

    def conv_silu(raw_ref, st_ref, w_ref, bias_ref, keep_ref):
        @pl.when(c == 0)
        def _():
            keep_ref[...] = st_ref[0]

        raw = raw_ref[...]
        ext = jnp.concatenate([keep_ref[...], raw], axis=0)
        w = w_ref[...]
        acc = bias_ref[...] + w[0:1] * ext[5:5 + valid]
        acc = acc + w[1:2] * ext[6:6 + valid]
        acc = acc + w[2:3] * ext[7:7 + valid]
        acc = acc + w[3:4] * raw
        keep_ref[...] = raw[valid - SUBLANES:]
        return pad(acc * jax.nn.sigmoid(acc))

    x_all = conv_silu(x_ref, sx_ref, wx_ref, bx_ref, kx_ref)
    b_all = conv_silu(b_ref, sb_ref, wb_ref, bb_ref, kb_ref)
    c_all = conv_silu(c_ref, sc_ref, wc_ref, bc_ref, kc_ref)

    ri = lax.broadcasted_iota(jnp.int32, (rows, rows), 0)
    ci = lax.broadcasted_iota(jnp.int32, (rows, rows), 1)
    tri = ri >= ci
    trif = tri.astype(F32)
    for gi in range(gpb):
        gs = slice(gi * gw, (gi + 1) * gw)
        ls = slice(gi * LANES, (gi + 1) * LANES)
        ss = slice(gi * ds, (gi + 1) * ds)

        @pl.when(c == 0)
        def _():
            ht_ref[gi] = h0_ref[0, gi * hpg:(gi + 1) * hpg].reshape(gw, ds).T

        x = x_all[:, gs]
        bm = b_all[:, ss]
        cm = c_all[:, ss]
        dt = pad(jax.nn.softplus(dt_ref[:, ls] + dtb_ref[:, ls]))
        da = dt * a_ref[:, ls]
        cum = jnp.dot(trif, da, precision=lax.Precision.HIGHEST,
                      preferred_element_type=F32)
        cum_t = cum.T
        ecum = jnp.exp(cum)
        toend = jnp.exp(cum[rows - 1:rows, :] - cum)
        dt_e, ecum_e, w_e = _expand([dt, ecum, dt * toend], hd, hpg, 2)
        (cum_e,) = _expand([cum], rows, hpg, 3)
        cb16 = cm.astype(BF16)
        cb = _dot_nt(cb16, bm.astype(BF16))
        ht = ht_ref[gi]
        yoff = _dot(cb16, ht.astype(BF16))
        xdt16 = (x * dt_e).astype(BF16)
        ys = []
        for h in range(hpg):
            dec = jnp.exp(jnp.where(tri, cum_e[:, h * rows:(h + 1) * rows] - cum_t[h:h + 1, :], -jnp.inf))
            ys.append(_dot((cb * dec).astype(BF16), xdt16[:, h * hd:(h + 1) * hd]))
        y = jnp.concatenate(ys, axis=1) + yoff * ecum_e
        ht_new = ht * ecum_e[rows - 1:rows, :] + _dot(bm.T.astype(BF16), (x * w_e).astype(BF16))
        ht_ref[gi] = ht_new

        z = pad(z_ref[:, gs])
        y = (y + x * d_ref[:, gs]) * (z * jax.nn.sigmoid(z))
        ms = jnp.mean(y * y, axis=-1, keepdims=True)
        yn = y * lax.rsqrt(ms + RMS_EPS) * ng_ref[:, gs]
        y_ref[:, gs] = yn[:valid].astype(y_ref.dtype)

        @pl.when(c == nc - 1)
        def _():
            hout_ref[0, gi * hpg:(gi + 1) * hpg] = ht_new.T.reshape((hpg,) + hout_ref.shape[2:])


def ssd_scan(xbc, state8, conv_w, conv_b, z, dt_raw, dtb, a, d_e, ng, h0, h0_off, n_seq, seq,
             d_inner, d_state, groups):
    n = xbc.shape[0]
    heads, hd = h0.shape[1], h0.shape[2]
    hpg = heads // groups
    gw = d_inner // groups
    gpb = SSD_GROUPS_PER_STEP
    assert gw == hpg * hd and d_state == LANES and gw % LANES == 0 and groups % gpb == 0
    rows = SSD_CHUNK
    valid = rows if seq % rows == 0 else seq
    assert valid == rows or seq < rows
    nc = seq // valid
    boff, coff = d_inner // (gpb * d_state), (d_inner // d_state + groups) // gpb
    row = lambda s, g, c: s * nc + c
    xw, bw = gpb * gw, gpb * d_state
    cmaps = ((xw, lambda g: g), (bw, lambda g: boff + g), (bw, lambda g: coff + g))

    def per_col(lead):
        return [pl.BlockSpec(tuple(lead) + (w,), lambda s, g, c, cm=cm, k=len(lead): (0,) * k + (cm(g),))
                for w, cm in cmaps]

    y, h_new = pl.pallas_call(
        functools.partial(_ssd_kernel, rows=rows, valid=valid, nc=nc, hpg=hpg, hd=hd, gpb=gpb),
        grid=(n_seq, groups // gpb, nc),
        in_specs=[pl.BlockSpec((valid, w), lambda s, g, c, cm=cm: (row(s, g, c), cm(g))) for w, cm in cmaps]
        + [pl.BlockSpec((1, SUBLANES, w), lambda s, g, c, cm=cm: (s, 0, cm(g))) for w, cm in cmaps]
        + per_col((CONV_W,)) + per_col((1,))
        + [pl.BlockSpec((valid, gpb * gw), lambda s, g, c: (row(s, g, c), g)),
                  pl.BlockSpec((valid, gpb * LANES), lambda s, g, c: (row(s, g, c), g)),
                  pl.BlockSpec((1, gpb * LANES), lambda s, g, c: (0, g)),
                  pl.BlockSpec((1, gpb * LANES), lambda s, g, c: (0, g)),
                  pl.BlockSpec((1, gpb * gw), lambda s, g, c: (0, g)),
                  pl.BlockSpec((1, gpb * gw), lambda s, g, c: (0, g)),
                  pl.BlockSpec((1, gpb * hpg, hd, d_state), lambda s, g, c: (h0_off + s, g, 0, 0))],
        out_specs=[pl.BlockSpec((valid, gpb * gw), lambda s, g, c: (row(s, g, c), g)),
                   pl.BlockSpec((1, gpb * hpg, hd, d_state), lambda s, g, c: (s, g, 0, 0))],
        out_shape=[jax.ShapeDtypeStruct((n, d_inner), BF16),
                   jax.ShapeDtypeStruct((n_seq,) + h0.shape[1:], F32)],
        scratch_shapes=[pltpu.VMEM((gpb, d_state, gw), F32), pltpu.VMEM((SUBLANES, xw), F32),
                        pltpu.VMEM((SUBLANES, bw), F32), pltpu.VMEM((SUBLANES, bw), F32)],
        compiler_params=_cp("parallel", "parallel", "arbitrary"),
        name="ssd_scan",
    )(xbc, xbc, xbc, state8, state8, state8, conv_w, conv_w, conv_w, conv_b, conv_b, conv_b,
      z, dt_raw, dtb, a, d_e, ng, h0)
    return y, h_new


def _lanes(v, w):
    return jnp.concatenate([v] * (w // LANES), axis=1) if w >= LANES else v[:, :w]


def _softmax_step(s, m_ref, l_ref, acc_ref, vb):
    m_old = m_ref[...]
    m_new = jnp.maximum(m_old, jnp.max(s, axis=-1, keepdims=True))
    p = jnp.exp2(s - _lanes(m_new, s.shape[1]))
    corr = jnp.exp2(m_old - m_new)
    l_ref[...] = corr * l_ref[...] + jnp.sum(p, axis=-1, keepdims=True)
    acc_ref[...] = _lanes(corr, acc_ref.shape[1]) * acc_ref[...] + _dot(p.astype(BF16), vb)
    m_ref[...] = m_new


def _diff_finish(lam, lam_init, g_ref, o_ref, l0, a0, l1, a1):
    w = a0.shape[1]
    o = a0[...] / _lanes(l0[...], w) - lam * (a1[...] / _lanes(l1[...], w))
    ms = jnp.mean(o * o, axis=-1, keepdims=True)
    o_ref[...] = (o * lax.rsqrt(ms + RMS_EPS) * g_ref[...] * (1.0 - lam_init)).astype(o_ref.dtype)


def _diff_init(m0, l0, a0, m1, l1, a1):
    for m, l, a in ((m0, l0, a0), (m1, l1, a1)):
        m[...] = jnp.full_like(m, -jnp.inf)
        l[...] = jnp.zeros_like(l)
        a[...] = jnp.zeros_like(a)


def _diff_prompt_kernel(lam_ref, q_ref, k_ref, v_ref, g_ref, o_ref, m0, l0, a0, m1, l1, a1,
                        *, tq, tk, hd, lam_init):
    qi = pl.program_id(2)
    _diff_init(m0, l0, a0, m1, l1, a1)
    q = q_ref[...]
    q0, q1 = q[:, :hd], q[:, hd:]
    per_q = tq // tk

    def block(ki, mask, r0=0):
        off = pl.multiple_of(ki * tk, tk)
        kb = k_ref[pl.ds(off, tk), :]
        vb = v_ref[pl.ds(off, tk), :]
        rows = pl.ds(r0, tq - r0)
        for qj, kj, m, l, a in ((q0, kb[:, :hd], m0, l0, a0), (q1, kb[:, hd:], m1, l1, a1)):
            s = _dot_nt(qj[r0:], kj)
            if mask is not None:
                s = jnp.where(mask, s, -jnp.inf)
            _softmax_step(s, m.at[rows], l.at[rows], a.at[rows], vb)

    def body(ki, carry):
        block(ki, None)
        return carry

    lax.fori_loop(0, qi * per_q, body, 0)
    for d in range(per_q):
        r0 = d * tk
        rc = (lax.broadcasted_iota(jnp.int32, (tq - r0, tk), 0) + r0) // MASK_CHUNK
        cc = (lax.broadcasted_iota(jnp.int32, (tq - r0, tk), 1) + r0) // MASK_CHUNK
        block(qi * per_q + d, cc <= rc, r0)
    _diff_finish(lam_ref[0], lam_init, g_ref, o_ref, l0, a0, l1, a1)


def diff_attn_prompt(q, k, v, lam, g, n_seq, seq, heads, lam_init):
    n, d = q.shape
    hw = d // heads
    hd = hw // 2
    tq = _tile(seq, TQ_DIFF, MASK_CHUNK)
    tk = _tile(tq, TK_DIFF, MASK_CHUNK)
    qb = seq // tq
    stat = pltpu.VMEM((tq, LANES), F32)
    acc = pltpu.VMEM((tq, hw), F32)
    return pl.pallas_call(
        functools.partial(_diff_prompt_kernel, tq=tq, tk=tk, hd=hd, lam_init=lam_init),
        grid=(n_seq, heads, qb),
        in_specs=[pl.BlockSpec(memory_space=pltpu.SMEM),
                  pl.BlockSpec((tq, hw), lambda b, h, i: (b * qb + i, h)),
                  pl.BlockSpec((seq, hw), lambda b, h, i: (b, h)),
                  pl.BlockSpec((seq, hw), lambda b, h, i: (b, h)),
                  pl.BlockSpec((1, hw), lambda b, h, i: (0, 0))],
        out_specs=pl.BlockSpec((tq, hw), lambda b, h, i: (b * qb + i, h)),
        out_shape=jax.ShapeDtypeStruct((n, d), BF16),
        scratch_shapes=[stat, stat, acc, stat, stat, acc],
        compiler_params=_cp("parallel", "parallel", "arbitrary"),
        name="diff_attn_prompt",
    )(lam, q, k, v, g.reshape(1, hw))


def _diff_sample_kernel(lam_ref, q_ref, kc_ref, vc_ref, kn_ref, vn_ref, g_ref, o_ref, m_ref, l_ref, a_ref,
                        *, heads, nkc, hd, lam_init):
    c = pl.program_id(1)
    hw = 2 * hd
    halves = hw // LANES

    @pl.when(c == 0)
    def _():
        m_ref[...] = jnp.full_like(m_ref, -jnp.inf)
        l_ref[...] = jnp.zeros_like(l_ref)
        a_ref[...] = jnp.zeros_like(a_ref)

    def step(h, j, kb, vb):
        i = 2 * h + j
        qj = q_ref[:, i * hd:(i + 1) * hd]
        _softmax_step(_dot_nt(qj, kb), m_ref.at[i], l_ref.at[i], a_ref.at[i], vb)

    @pl.when(c < nkc)
    def _():
        for h in range(heads):
            vb = jnp.concatenate([vc_ref[:, e * heads + h, :] for e in range(halves)], axis=1).astype(BF16)
            for j in (0, 1):
                step(h, j, kc_ref[:, 2 * h + j, :].astype(BF16), vb)

    @pl.when(c == nkc)
    def _():
        for h in range(heads):
            vb = vn_ref[:, h * hw:(h + 1) * hw]
            for j in (0, 1):
                step(h, j, kn_ref[:, (2 * h + j) * hd:(2 * h + j + 1) * hd], vb)
            _diff_finish(lam_ref[0], lam_init, g_ref, o_ref.at[:, h * hw:(h + 1) * hw],
                         l_ref.at[2 * h], a_ref.at[2 * h], l_ref.at[2 * h + 1], a_ref.at[2 * h + 1])


def diff_attn_sample(q, k_new, v_new, k_cache, v_cache, cache_off, past, lam, g, n_seq, seq, heads, lam_init):
    n, d = q.shape
    hw = d // heads
    hd = hw // 2
    nch = d // LANES
    tk = _tile(past, TK_DIFF, SUBLANES)
    nkc = past // tk
    rows = pl.BlockSpec((seq, d), lambda b, c: (b, 0))
    cache = pl.BlockSpec((tk, nch, LANES), lambda b, c: ((cache_off + b) * nkc + jnp.minimum(c, nkc - 1), 0, 0))
    return pl.pallas_call(
        functools.partial(_diff_sample_kernel, heads=heads, nkc=nkc, hd=hd, lam_init=lam_init),
        grid=(n_seq, nkc + 1),
        in_specs=[pl.BlockSpec(memory_space=pltpu.SMEM), rows, cache, cache, rows, rows,
                  pl.BlockSpec((1, hw), lambda b, c: (0, 0))],
        out_specs=rows,
        out_shape=jax.ShapeDtypeStruct((n, d), BF16),
        scratch_shapes=[pltpu.VMEM((2 * heads, seq, LANES), F32), pltpu.VMEM((2 * heads, seq, LANES), F32),
                        pltpu.VMEM((2 * heads, seq, hw), F32)],
        compiler_params=_cp("parallel", "arbitrary"),
        name="diff_attn_sample",
    )(lam, q, k_cache, v_cache, k_new, v_new, g.reshape(1, hw))


def _mem_attn_kernel(q_ref, k_ref, v_ref, o_ref, *, heads, hd, scale):
    for h in range(heads):
        sl = slice(h * hd, (h + 1) * hd)
        s = _dot_nt(q_ref[:, sl], k_ref[:, sl].astype(BF16)) * scale
        p = jnp.exp(s - jnp.max(s, axis=-1, keepdims=True))
        l = jnp.sum(p, axis=-1, keepdims=True)
        o = _dot(p.astype(BF16), v_ref[:, sl].astype(BF16)) / l
        o_ref[:, sl] = o.astype(o_ref.dtype)


def mem_attn(q, mk, mv, kv_off, m, n_seq, seq, heads):
    n, d = q.shape
    hd = d // heads
    tq = _tile(seq, TQ_MEM, 16)
    qb = seq // tq
    kv = pl.BlockSpec((m, d), lambda b, i: (kv_off + b, 0))
    return pl.pallas_call(
        functools.partial(_mem_attn_kernel, heads=heads, hd=hd, scale=hd ** -0.5),
        grid=(n_seq, qb),
        in_specs=[pl.BlockSpec((tq, d), lambda b, i: (b * qb + i, 0)), kv, kv],
        out_specs=pl.BlockSpec((tq, d), lambda b, i: (b * qb + i, 0)),
        out_shape=jax.ShapeDtypeStruct((n, d), BF16),
        compiler_params=_cp("parallel", "parallel"),
        name="mem_attn",
    )(q, mk, mv)


def _top_rows(v, k):
    rows = []
    for _ in range(k):
        m = jnp.max(v, axis=0, keepdims=True)
        rows.append(m)
        v = jnp.where(v == m, -jnp.inf, v)
    return rows


def _sort_network(lo, hi):
    def merge(lo, hi, r):
        step = r * 2
        if step < hi - lo:
            yield from merge(lo, hi, step)
            yield from merge(lo + r, hi, step)
            yield from [(i, i + r) for i in range(lo + r, hi - r, step)]
        else:
            yield (lo, lo + r)

    if hi - lo >= 1:
        mid = lo + (hi - lo) // 2
        yield from _sort_network(lo, mid)
        yield from _sort_network(mid + 1, hi)
        yield from merge(lo, hi, 1)


def _top_sorted(v):
    n = v.shape[0] // SUBLANES
    x = [v[r * SUBLANES:(r + 1) * SUBLANES] for r in range(n)]

    def order(i, j):
        x[i], x[j] = jnp.maximum(x[i], x[j]), jnp.minimum(x[i], x[j])

    for i, j in _sort_network(0, n - 1):
        order(i, j)
    shift = SUBLANES // 2
    while shift >= 1:
        x = [jnp.maximum(x[r], pltpu.roll(x[n - 1 - r], shift, 0)) for r in range(n)]
        d = n // 2
        while d >= 1:
            for i in range(n):
                if i & d == 0:
                    order(i, i + d)
            d //= 2
        shift //= 2
    return [xr[0:1] for xr in x]


def _peer_prep_kernel(xt_ref, wq_ref, keys_ref, s0_ref, e0_ref, s1_ref, e1_ref, tau_ref, *, heads, gpc):
    k = PEER_TOPK
    qt = _dot(wq_ref[...], xt_ref[...]).astype(BF16)
    half = keys_ref.shape[2]
    for h in range(heads):
        sc = [_dot(keys_ref[2 * h + j], qt[(2 * h + j) * half:(2 * h + j + 1) * half]) for j in (0, 1)]
        a = _top_sorted(sc[0])
        b = jnp.concatenate(_top_sorted(sc[1]), axis=0)
        cand = [a[0] + b] + [a[r] + b[:k // 2] for r in range(1, k // 2)]
        cand.append(jnp.concatenate(a[k // 2:], axis=0) + b[0:1])
        cand = jnp.concatenate(cand, axis=0)
        tau = _top_rows(cand, k)[-1]
        mx = a[0] + b[0:1]
        z = jnp.sum(jnp.where(cand >= tau, jnp.exp(cand - mx), 0.0), axis=0, keepdims=True)
        e0 = jnp.exp(sc[0] - a[0])
        for c in range(s0_ref.shape[1]):
            s0_ref[h, c] = sc[0][c * gpc:(c + 1) * gpc]
            e0_ref[h, c] = e0[c * gpc:(c + 1) * gpc]
        s1_ref[h] = sc[1]
        e1_ref[h] = jnp.exp(sc[1] - b[0:1]) / z
        tau_ref[h:h + 1, :] = tau


def peer_prep(xt, wq_t, keys, gpc):
    d, n = xt.shape
    hj, nk, half = keys.shape
    heads = hj // 2
    assert PEER_TOPK % 2 == 0 and nk % gpc == 0 and nk == SUBLANES * PEER_TOPK
    t = _tile(n, T_PEER, LANES)
    first = pl.BlockSpec((heads, nk // gpc, gpc, t), lambda i: (0, 0, 0, i))
    second = pl.BlockSpec((heads, nk, t), lambda i: (0, 0, i))
    f_shape = jax.ShapeDtypeStruct((heads, nk // gpc, gpc, n), F32)
    s_shape = jax.ShapeDtypeStruct((heads, nk, n), F32)
    return pl.pallas_call(
        functools.partial(_peer_prep_kernel, heads=heads, gpc=gpc),
        grid=(n // t,),
        in_specs=[pl.BlockSpec((d, t), lambda i: (0, i)),
                  pl.BlockSpec(wq_t.shape, lambda i: (0, 0)),
                  pl.BlockSpec(keys.shape, lambda i: (0, 0, 0))],
        out_specs=[first, first, second, second, pl.BlockSpec((heads, t), lambda i: (0, i))],
        out_shape=[f_shape, f_shape, s_shape, s_shape, jax.ShapeDtypeStruct((heads, n), F32)],
        compiler_params=_cp("parallel"),
        name="peer_prep",
    )(xt, wq_t, keys)


def _gelu_tanh(x):
    return 0.5 * x * (1.0 + jnp.tanh(math.sqrt(2.0 / math.pi) * (x + 0.044715 * (x * x * x))))


def _peer_mix(s0_ref, e0_ref, s1_ref, e1_ref, tau_ref, act_ref, h_ref, *, heads, nk, t):
    gpc = act_ref.shape[0] // nk
    jh = PEER_J_ROWS
    for tl in range(t // LANES):
        ls = slice(tl * LANES, (tl + 1) * LANES)
        for jb in range(nk // jh):
            js = slice(jb * jh, (jb + 1) * jh)
            gates = [jnp.zeros((jh, LANES), F32) for _ in range(gpc)]
            for h in range(heads):
                s1, e1, tau = s1_ref[h, js, ls], e1_ref[h, js, ls], tau_ref[h:h + 1, ls]
                for ii in range(gpc):
                    hit = s0_ref[h, 0, ii:ii + 1, ls] + s1 >= tau
                    gates[ii] = gates[ii] + jnp.where(hit, e0_ref[h, 0, ii:ii + 1, ls] * e1, 0.0)
            for ii in range(gpc):
                rows = slice(ii * nk + jb * jh, ii * nk + (jb + 1) * jh)
                h_ref[rows, ls] = (gates[ii] * _gelu_tanh(act_ref[rows, ls])).astype(BF16)


def _peer_main_kernel(xt_ref, u_ref, vt_ref, s0a_ref, e0a_ref, s0b_ref, e0b_ref, s1_ref, e1_ref, tau_ref,
                      x_ref, lg_ref, lb_ref, o_ref, acc_ref, act_a, act_b, h_a, h_b,
                      *, heads, nk, ns, t, alpha):
    g = pl.program_id(1)
    ec = act_a.shape[0]
    mix = functools.partial(_peer_mix, s1_ref=s1_ref, e1_ref=e1_ref, tau_ref=tau_ref, heads=heads, nk=nk, t=t)

    @pl.when(g == 0)
    def _():
        acc_ref[...] = jnp.zeros_like(acc_ref)
        act_b[...] = jnp.zeros_like(act_b)
        h_a[...] = jnp.zeros_like(h_a)
        h_b[...] = jnp.zeros_like(h_b)

    xt = xt_ref[...]
    out_a = _dot(vt_ref[:, :ec], h_a[...])
    mix(s0b_ref, e0b_ref, act_ref=act_b, h_ref=h_b)
    act_a[...] = _dot(u_ref[:ec, :], xt)
    out_b = _dot(vt_ref[:, ec:], h_b[...])
    mix(s0a_ref, e0a_ref, act_ref=act_a, h_ref=h_a)
    act_b[...] = _dot(u_ref[ec:, :], xt)
    acc_ref[...] += out_a + out_b

    @pl.when(g == ns - 1)
    def _():
        o_ref[...] = _layer_norm(alpha * x_ref[...] + acc_ref[...].T, lg_ref[...], lb_ref[...])


def peer_main(xt, u, vt, s0, e0, s1, e1, tau, x, ln_g, ln_b, alpha):
    d, n = xt.shape
    heads, nchunk, gpc, _ = s0.shape
    nk = s1.shape[1]
    n_exp = u.shape[0]
    ec = gpc * nk
    assert n_exp == nchunk * ec and nchunk % 2 == 0
    t = _tile(n, T_PEER, LANES)
    ns = nchunk // 2 + 1
    last = nchunk // 2 - 1
    first_a = pl.BlockSpec((heads, 1, gpc, t), lambda i, g: (0, jnp.minimum(2 * g, nchunk - 1), 0, i))
    first_b = pl.BlockSpec((heads, 1, gpc, t), lambda i, g: (0, jnp.maximum(2 * g - 1, 0), 0, i))
    once = pl.Buffered(1)
    second = pl.BlockSpec((heads, nk, t), lambda i, g: (0, 0, i), pipeline_mode=once)
    return pl.pallas_call(
        functools.partial(_peer_main_kernel, heads=heads, nk=nk, ns=ns, t=t, alpha=alpha),
        grid=(n // t, ns),
        in_specs=[pl.BlockSpec((d, t), lambda i, g: (0, i)),
                  pl.BlockSpec((2 * ec, d), lambda i, g: (jnp.minimum(g, last), 0)),
                  pl.BlockSpec((d, 2 * ec), lambda i, g: (0, jnp.maximum(g - 1, 0))),
                  first_a, first_a, first_b, first_b, second, second,
                  pl.BlockSpec((heads, t), lambda i, g: (0, i)),
                  pl.BlockSpec((t, d), lambda i, g: (i, 0), pipeline_mode=once),
                  pl.BlockSpec((1, d), lambda i, g: (0, 0)),
                  pl.BlockSpec((1, d), lambda i, g: (0, 0))],
        out_specs=pl.BlockSpec((t, d), lambda i, g: (i, 0)),
        out_shape=jax.ShapeDtypeStruct((n, d), F32),
        scratch_shapes=[pltpu.VMEM((d, t), F32), pltpu.VMEM((ec, t), F32), pltpu.VMEM((ec, t), F32),
                        pltpu.VMEM((ec, t), BF16), pltpu.VMEM((ec, t), BF16)],
        compiler_params=_cp("parallel", "arbitrary"),
        name="peer_main",
    )(xt, u, vt, s0, e0, s0, e0, s1, e1, tau, x, ln_g.reshape(1, d), ln_b.reshape(1, d))


def _diff_lambda(lam_vecs, layer_idx):
    lam_init = 0.8 - 0.6 * math.exp(-0.3 * layer_idx)
    lv = lam_vecs.astype(F32)
    lam = jnp.exp(jnp.sum(lv[0] * lv[1])) - jnp.exp(jnp.sum(lv[2] * lv[3])) + lam_init
    return lam.reshape(1), lam_init


def kernel(x_prompt, x_sample, state_ssd_conv, state_ssd_h, cache_diff_k, cache_diff_v, cache_mem_k, cache_mem_v, mem_prompt, ln_g, ln_b, ssd_w_in, ssd_conv_w, ssd_conv_b, ssd_a_log, ssd_dt_bias, ssd_d, ssd_norm_g, ssd_w_out, diff_w_qkv, diff_lam, diff_subln_g, diff_w_o, mem_w_q, mem_w_kv, mem_w_o, peer_w_q, peer_keys, peer_u, peer_v):
    bp, lp, d = x_prompt.shape
    bs, ls, _ = x_sample.shape
    depth = ln_g.shape[0]
    alpha = (2.0 * depth) ** 0.25
    n_mixers = 2
    ssd_heads = ssd_a_log.shape[1]
    d_inner = ssd_norm_g.shape[1]
    conv_dim = ssd_conv_w.shape[2]
    d_state = state_ssd_h.shape[-1]
    groups = (conv_dim - d_inner) // (2 * d_state)
    hpg = ssd_heads // groups
    hd_ssd = d_inner // ssd_heads
    diff_heads = cache_diff_k.shape[3]
    mem_tokens, mem_heads = cache_mem_k.shape[2], cache_mem_k.shape[3]
    peer_heads, _, n_keys, peer_half = peer_keys.shape[1:]

    streams = [
        dict(x=x_prompt.reshape(bp * lp, d), n_seq=bp, seq=lp),
        dict(x=x_sample.reshape(bs * ls, d), n_seq=bs, seq=ls),
    ]
    outs = [dict(conv=[], h=[], dk=[], dv=[]), dict(conv=[], h=[], dk=[], dv=[])]
    mk_p, mv_p = [], []

    def group_lanes(v):
        v = v.reshape(v.shape[:-1] + (groups, hpg))
        v = jnp.pad(v, [(0, 0)] * (v.ndim - 1) + [(0, LANES - hpg)])
        return v.reshape(v.shape[:-2] + (groups * LANES,))

    for i in range(depth):
        j = i // n_mixers
        g0, b0 = ln_g[i, 0], ln_b[i, 0]
        if i % n_mixers == 0:
            w_in = ssd_w_in[j]
            w_z = w_in[:, :d_inner].astype(BF16)
            w_xbc = w_in[:, d_inner:d_inner + conv_dim].astype(BF16)
            w_dt = group_lanes(w_in[:, d_inner + conv_dim:]).astype(BF16)
            dtb = group_lanes(ssd_dt_bias[j]).reshape(1, -1)
            a_neg = group_lanes(-jnp.exp(ssd_a_log[j].astype(F32))).reshape(1, -1)
            d_e = jnp.repeat(ssd_d[j], hd_ssd).reshape(1, d_inner)
            ng = ssd_norm_g[j].reshape(1, d_inner)
            w_out = ssd_w_out[j].astype(BF16)
            for si, st in enumerate(streams):
                x, n_seq, seq = st["x"], st["n_seq"], st["seq"]
                z = matmul(x, w_z)
                xbc = matmul(x, w_xbc)
                dt_raw = matmul(x, w_dt)
                if si == 0:
                    state8 = jnp.zeros((n_seq, SUBLANES, conv_dim), F32)
                    h0, h0_off = jnp.zeros((n_seq, ssd_heads, hd_ssd, d_state), F32), 0
                else:
                    state8 = jnp.pad(state_ssd_conv[j], ((0, 0), (SUBLANES - (CONV_W - 1), 0), (0, 0)))
                    h0, h0_off = state_ssd_h.reshape((-1,) + state_ssd_h.shape[2:]), j * n_seq
                outs[si]["conv"].append(xbc.reshape(n_seq, seq, conv_dim)[:, seq - (CONV_W - 1):])
                y, h_new = ssd_scan(xbc, state8, ssd_conv_w[j], ssd_conv_b[j].reshape(1, conv_dim), z, dt_raw,
                                    dtb, a_neg, d_e, ng, h0, h0_off, n_seq, seq, d_inner, d_state, groups)
                outs[si]["h"].append(h_new)
                st["x"] = matmul_ln(y, w_out, x, g0, b0, alpha)
        else:
            lam, lam_init = _diff_lambda(diff_lam[j], i)
            w_qkv = diff_w_qkv[j]
            w_q, w_k, w_v = (w_qkv[:, c * d:(c + 1) * d].astype(BF16) for c in range(3))
            w_o = diff_w_o[j].astype(BF16)
            for si, st in enumerate(streams):
                x, n_seq, seq = st["x"], st["n_seq"], st["seq"]
                hd_diff = d // diff_heads // 2
                q = matmul(x, w_q, (BF16,), scale=hd_diff ** -0.5 * math.log2(math.e))
                hw = d // diff_heads
                halves = hw // LANES
                k32, k16 = matmul_rows(x, w_k, range(d // LANES))
                v32, v16 = matmul_rows(x, w_v, [(c % halves) * diff_heads + c // halves for c in range(d // LANES)])
                outs[si]["dk"].append(k32.reshape(n_seq, seq, diff_heads, 2, hw // 2))
                outs[si]["dv"].append(v32.reshape(n_seq, seq, halves, diff_heads, LANES)
                                      .transpose(0, 1, 3, 2, 4).reshape(n_seq, seq, diff_heads, hw))
                if si == 0:
                    o = diff_attn_prompt(q, k16, v16, lam, diff_subln_g[j], n_seq, seq, diff_heads, lam_init)
                else:
                    past = cache_diff_k.shape[2]
                    kc = cache_diff_k.reshape(-1, d // LANES, LANES)
                    vc = (cache_diff_v.reshape(-1, diff_heads, halves, LANES).transpose(0, 2, 1, 3)
                          .reshape(-1, d // LANES, LANES))
                    o = diff_attn_sample(q, k16, v16, kc, vc, j * n_seq, past, lam, diff_subln_g[j],
                                         n_seq, seq, diff_heads, lam_init)
                st["x"] = matmul_ln(o, w_o, x, g0, b0, alpha)

        w_mq = mem_w_q[i].astype(BF16)
        w_mk = mem_w_kv[i][:, :d].astype(BF16)
        w_mv = mem_w_kv[i][:, d:].astype(BF16)
        w_mo = mem_w_o[i].astype(BF16)
        mem_rows = mem_prompt.reshape(bp * mem_tokens, d)
        mk = matmul(mem_rows, w_mk)
        mv = matmul(mem_rows, w_mv)
        mk_p.append(mk.reshape(bp, mem_tokens, mem_heads, d // mem_heads))
        mv_p.append(mv.reshape(bp, mem_tokens, mem_heads, d // mem_heads))
        wq_t = peer_w_q[i].T.astype(BF16)
        keys = peer_keys[i].reshape(peer_heads * 2, n_keys, peer_half).astype(BF16)
        u = peer_u[i].astype(BF16)
        vt = peer_v[i].T.astype(BF16)
        for si, st in enumerate(streams):
            x, n_seq, seq = st["x"], st["n_seq"], st["seq"]
            q = matmul(x, w_mq, (BF16,))
            if si == 0:
                kk, vv, kv_off = mk, mv, 0
            else:
                kk, vv, kv_off = cache_mem_k.reshape(-1, d), cache_mem_v.reshape(-1, d), i * n_seq
            o = mem_attn(q, kk, vv, kv_off, mem_tokens, n_seq, seq, mem_heads)
            x, xt = matmul_ln(o, w_mo, x, ln_g[i, 1], ln_b[i, 1], alpha, with_t=True)
            s0, e0, s1, e1, tau = peer_prep(xt, wq_t, keys, EC_PEER // n_keys)
            st["x"] = peer_main(xt, u, vt, s0, e0, s1, e1, tau, x, ln_g[i, 2], ln_b[i, 2], alpha)

    yp = streams[0]["x"].reshape(bp, lp, d)
    ys = streams[1]["x"].reshape(bs, ls, d)
    o_p, o_s = outs
    return (yp, ys, jnp.stack(o_p["conv"]), jnp.stack(o_p["h"]), jnp.stack(o_p["dk"]), jnp.stack(o_p["dv"]),
            jnp.stack(mk_p), jnp.stack(mv_p), jnp.stack(o_s["conv"]), jnp.stack(o_s["h"]),
            jnp.stack(o_s["dk"]), jnp.stack(o_s["dv"]))
```

```python
import functools
import math

import jax
import jax.numpy as jnp
from jax import lax
from jax.experimental import pallas as pl
from jax.experimental.pallas import tpu as pltpu

F32 = jnp.float32
BF16 = jnp.bfloat16

LANES = 128
SUBLANES = 8
VMEM_LIMIT_BYTES = 56 * 1024 * 1024

MASK_CHUNK = 64
CONV_W = 4
PEER_TOPK = 16
LN_EPS = 1e-5
RMS_EPS = 1e-5
SSD_CHUNK = 128
SSD_GROUPS_PER_STEP = 4

TM = 1024
TN = 512
TM_LN = 512
TK = 2048
TQ_DIFF = 1024
TK_DIFF = 512
TQ_MEM = 512
T_PEER = 512
EC_PEER = 512
PEER_J_ROWS = 32

_NT = (((1,), (1,)), ((), ()))


def _tile(n, target, mult):
    t = (min(target, n) // mult) * mult
    while t >= mult:
        if n % t == 0:
            return t
        t -= mult
    return n


def _cp(*sem):
    return pltpu.CompilerParams(dimension_semantics=sem, vmem_limit_bytes=VMEM_LIMIT_BYTES)


def _dot(a, b):
    return jnp.dot(a, b, preferred_element_type=F32)


def _dot_nt(a, b):
    return lax.dot_general(a, b, _NT, preferred_element_type=F32)


def _layer_norm(h, g, b):
    mu = jnp.mean(h, axis=-1, keepdims=True)
    d = h - mu
    var = jnp.mean(d * d, axis=-1, keepdims=True)
    return d * lax.rsqrt(var + LN_EPS) * g + b


def _mm_kernel(a_ref, w_ref, *o_refs, scale):
    acc = _dot(a_ref[...].astype(BF16), w_ref[...])
    if scale is not None:
        acc = acc * scale
    for o in o_refs:
        o[...] = acc.astype(o.dtype)


def matmul(a, w, out_dtypes=(F32,), scale=None):
    n, k = a.shape
    m = w.shape[1]
    tm, tn = _tile(n, TM, 16), _tile(m, TN, LANES)
    outs = pl.pallas_call(
        functools.partial(_mm_kernel, scale=scale),
        grid=(n // tm, m // tn),
        in_specs=[pl.BlockSpec((tm, k), lambda i, j: (i, 0)),
                  pl.BlockSpec((k, tn), lambda i, j: (0, j))],
        out_specs=[pl.BlockSpec((tm, tn), lambda i, j: (i, j)) for _ in out_dtypes],
        out_shape=[jax.ShapeDtypeStruct((n, m), d) for d in out_dtypes],
        compiler_params=_cp("parallel", "parallel"),
        name="matmul",
    )(a, w)
    return outs[0] if len(out_dtypes) == 1 else outs


def _mm_rows_kernel(a_ref, w_ref, lin_ref, o16_ref, *, tm, order):
    acc = _dot(a_ref[...].astype(BF16), w_ref[...])
    o16_ref[...] = acc.astype(BF16)
    nchunk = len(order)
    for c, r in enumerate(order):
        lin_ref[pl.ds(r, tm, stride=nchunk), :] = acc[:, c * LANES:(c + 1) * LANES]


def matmul_rows(a, w, order):
    n, k = a.shape
    m = w.shape[1]
    nchunk = m // LANES
    tm = _tile(n, TM_LN, 16)
    lin, o16 = pl.pallas_call(
        functools.partial(_mm_rows_kernel, tm=tm, order=tuple(order)),
        grid=(n // tm,),
        in_specs=[pl.BlockSpec((tm, k), lambda i: (i, 0)),
                  pl.BlockSpec((k, m), lambda i: (0, 0))],
        out_specs=[pl.BlockSpec((tm * nchunk, LANES), lambda i: (i, 0)),
                   pl.BlockSpec((tm, m), lambda i: (i, 0))],
        out_shape=[jax.ShapeDtypeStruct((n * nchunk, LANES), F32), jax.ShapeDtypeStruct((n, m), BF16)],
        compiler_params=_cp("parallel"),
        name="matmul_rows",
    )(a, w)
    return lin, o16


def _mm_ln_kernel(a_ref, w_ref, x_ref, g_ref, b_ref, *refs, alpha, nk, with_t):
    o_ref = refs[0]
    acc_ref = refs[-1] if nk > 1 else None
    k = pl.program_id(1)
    part = _dot(a_ref[...].astype(BF16), w_ref[...])

    def finish(f):
        y = _layer_norm(alpha * x_ref[...] + f, g_ref[...], b_ref[...])
        o_ref[...] = y
        if with_t:
            refs[1][...] = y.T.astype(BF16)

    if nk == 1:
        finish(part)
        return

    @pl.when(k == 0)
    def _():
        acc_ref[...] = part

    @pl.when(jnp.logical_and(k > 0, k < nk - 1))
    def _():
        acc_ref[...] += part

    @pl.when(k == nk - 1)
    def _():
        finish(acc_ref[...] + part)


def matmul_ln(a, w, x, g, b, alpha, with_t=False):
    n, kdim = a.shape
    d = w.shape[1]
    tm, tk = _tile(n, TM_LN, LANES if with_t else 16), _tile(kdim, TK, LANES)
    nk = kdim // tk
    out_shape = [jax.ShapeDtypeStruct((n, d), F32)]
    out_specs = [pl.BlockSpec((tm, d), lambda i, k: (i, 0))]
    if with_t:
        out_shape.append(jax.ShapeDtypeStruct((d, n), BF16))
        out_specs.append(pl.BlockSpec((d, tm), lambda i, k: (0, i)))
    outs = pl.pallas_call(
        functools.partial(_mm_ln_kernel, alpha=alpha, nk=nk, with_t=with_t),
        grid=(n // tm, nk),
        in_specs=[pl.BlockSpec((tm, tk), lambda i, k: (i, k)),
                  pl.BlockSpec((tk, d), lambda i, k: (k, 0)),
                  pl.BlockSpec((tm, d), lambda i, k: (i, 0)),
                  pl.BlockSpec((1, d), lambda i, k: (0, 0)),
                  pl.BlockSpec((1, d), lambda i, k: (0, 0))],
        out_specs=out_specs,
        out_shape=out_shape,
        scratch_shapes=[pltpu.VMEM((tm, d), F32)] if nk > 1 else [],
        compiler_params=_cp("parallel", "arbitrary"),
        name="matmul_ln",
    )(a, w, x, g.reshape(1, d), b.reshape(1, d))
    return outs if with_t else outs[0]


def _split_bf16(v, parts):
    out, r = [], v
    for _ in range(parts - 1):
        p = r.astype(BF16)
        out.append(p)
        r = r - p.astype(F32)
    out.append(r.astype(BF16))
    return out


def _expand(vals, width, hpg, parts):
    rows = vals[0].shape[0]
    lhs = jnp.concatenate([p for v in vals for p in _split_bf16(v, parts)], axis=0)
    k = lax.broadcasted_iota(jnp.int32, (LANES, hpg * width), 0)
    c = lax.broadcasted_iota(jnp.int32, (LANES, hpg * width), 1) // width
    res = _dot(lhs, jnp.where(k == c, 1.0, 0.0).astype(BF16))
    out = []
    for i in range(len(vals)):
        acc = res[i * parts * rows:(i * parts + 1) * rows]
        for j in range(1, parts):
            acc = acc + res[(i * parts + j) * rows:(i * parts + j + 1) * rows]
        out.append(acc)
    return out


def _ssd_kernel(x_ref, b_ref, c_ref, sx_ref, sb_ref, sc_ref, wx_ref, wb_ref, wc_ref, bx_ref, bb_ref, bc_ref,
                z_ref, dt_ref, dtb_ref, a_ref, d_ref, ng_ref, h0_ref,
                y_ref, hout_ref, ht_ref, kx_ref, kb_ref, kc_ref, *, rows, valid, nc, hpg, hd, gpb):
    c = pl.program_id(2)
    gw = hpg * hd
    ds = b_ref.shape[1] // gpb

    def pad(v):
        if valid == rows:
            return v
        return jnp.concatenate([v, jnp.zeros((rows - valid, v.shape[1]), v.dtype)], axis=0)

    def conv_silu(raw_ref, st_ref, w_ref, bias_ref, keep_ref):
        @pl.when(c == 0)
        def _():
            keep_ref[...] = st_ref[0]

        raw = raw_ref[...]
        ext = jnp.concatenate([keep_ref[...], raw], axis=0)
        w = w_ref[...]
        acc = bias_ref[...] + w[0:1] * ext[5:5 + valid]
        acc = acc + w[1:2] * ext[6:6 + valid]
        acc = acc + w[2:3] * ext[7:7 + valid]
        acc = acc + w[3:4] * raw
        keep_ref[...] = raw[valid - SUBLANES:]
        return pad(acc * jax.nn.sigmoid(acc))

    x_all = conv_silu(x_ref, sx_ref, wx_ref, bx_ref, kx_ref)
    b_all = conv_silu(b_ref, sb_ref, wb_ref, bb_ref, kb_ref)
    c_all = conv_silu(c_ref, sc_ref, wc_ref, bc_ref, kc_ref)

    ri = lax.broadcasted_iota(jnp.int32, (rows, rows), 0)
    ci = lax.broadcasted_iota(jnp.int32, (rows, rows), 1)
    tri = ri >= ci
    trif = tri.astype(F32)
    for gi in range(gpb):
        gs = slice(gi * gw, (gi + 1) * gw)
        ls = slice(gi * LANES, (gi + 1) * LANES)
        ss = slice(gi * ds, (gi + 1) * ds)

        @pl.when(c == 0)
        def _():
            ht_ref[gi] = h0_ref[0, gi * hpg:(gi + 1) * hpg].reshape(gw, ds).T

        x = x_all[:, gs]
        bm = b_all[:, ss]
        cm = c_all[:, ss]
        dt = pad(jax.nn.softplus(dt_ref[:, ls] + dtb_ref[:, ls]))
        da = dt * a_ref[:, ls]
        cum = jnp.dot(trif, da, precision=lax.Precision.HIGHEST,
                      preferred_element_type=F32)
        cum_t = cum.T
        ecum = jnp.exp(cum)
        toend = jnp.exp(cum[rows - 1:rows, :] - cum)
        dt_e, ecum_e, w_e = _expand([dt, ecum, dt * toend], hd, hpg, 2)
        (cum_e,) = _expand([cum], rows, hpg, 3)
        cb16 = cm.astype(BF16)
        cb = _dot_nt(cb16, bm.astype(BF16))
        ht = ht_ref[gi]
        yoff = _dot(cb16, ht.astype(BF16))
        xdt16 = (x * dt_e).astype(BF16)
        ys = []
        for h in range(hpg):
            dec = jnp.exp(jnp.where(tri, cum_e[:, h * rows:(h + 1) * rows] - cum_t[h:h + 1, :], -jnp.inf))
            ys.append(_dot((cb * dec).astype(BF16), xdt16[:, h * hd:(h + 1) * hd]))
        y = jnp.concatenate(ys, axis=1) + yoff * ecum_e
        ht_new = ht * ecum_e[rows - 1:rows, :] + _dot(bm.T.astype(BF16), (x * w_e).astype(BF16))
        ht_ref[gi] = ht_new

        z = pad(z_ref[:, gs])
        y = (y + x * d_ref[:, gs]) * (z * jax.nn.sigmoid(z))
        ms = jnp.mean(y * y, axis=-1, keepdims=True)
        yn = y * lax.rsqrt(ms + RMS_EPS) * ng_ref[:, gs]
        y_ref[:, gs] = yn[:valid].astype(y_ref.dtype)

        @pl.when(c == nc - 1)
        def _():
            hout_ref[0, gi * hpg:(gi + 1) * hpg] = ht_new.T.reshape((hpg,) + hout_ref.shape[2:])


def ssd_scan(xbc, state8, conv_w, conv_b, z, dt_raw, dtb, a, d_e, ng, h0, h0_off, n_seq, seq,
             d_inner, d_state, groups):
    n = xbc.shape[0]
    heads, hd = h0.shape[1], h0.shape[2]
    hpg = heads // groups
    gw = d_inner // groups
    gpb = SSD_GROUPS_PER_STEP
    assert gw == hpg * hd and d_state == LANES and gw % LANES == 0 and groups % gpb == 0
    rows = SSD_CHUNK
    valid = rows if seq % rows == 0 else seq
    assert valid == rows or seq < rows
    nc = seq // valid
    boff, coff = d_inner // (gpb * d_state), (d_inner // d_state + groups) // gpb
    row = lambda s, g, c: s * nc + c
    xw, bw = gpb * gw, gpb * d_state
    cmaps = ((xw, lambda g: g), (bw, lambda g: boff + g), (bw, lambda g: coff + g))

    def per_col(lead):
        return [pl.BlockSpec(tuple(lead) + (w,), lambda s, g, c, cm=cm, k=len(lead): (0,) * k + (cm(g),))
                for w, cm in cmaps]

    y, h_new = pl.pallas_call(
        functools.partial(_ssd_kernel, rows=rows, valid=valid, nc=nc, hpg=hpg, hd=hd, gpb=gpb),
        grid=(n_seq, groups // gpb, nc),
        in_specs=[pl.BlockSpec((valid, w), lambda s, g, c, cm=cm: (row(s, g, c), cm(g))) for w, cm in cmaps]
        + [pl.BlockSpec((1, SUBLANES, w), lambda s, g, c, cm=cm: (s, 0, cm(g))) for w, cm in cmaps]
        + per_col((CONV_W,)) + per_col((1,))
        + [pl.BlockSpec((valid, gpb * gw), lambda s, g, c: (row(s, g, c), g)),
                  pl.BlockSpec((valid, gpb * LANES), lambda s, g, c: (row(s, g, c), g)),
                  pl.BlockSpec((1, gpb * LANES), lambda s, g, c: (0, g)),
                  pl.BlockSpec((1, gpb * LANES), lambda s, g, c: (0, g)),
                  pl.BlockSpec((1, gpb * gw), lambda s, g, c: (0, g)),
                  pl.BlockSpec((1, gpb * gw), lambda s, g, c: (0, g)),
                  pl.BlockSpec((1, gpb * hpg, hd, d_state), lambda s, g, c: (h0_off + s, g, 0, 0))],
        out_specs=[pl.BlockSpec((valid, gpb * gw), lambda s, g, c: (row(s, g, c), g)),
                   pl.BlockSpec((1, gpb * hpg, hd, d_state), lambda s, g, c: (s, g, 0, 0))],
        out_shape=[jax.ShapeDtypeStruct((n, d_inner), BF16),
                   jax.ShapeDtypeStruct((n_seq,) + h0.shape[1:], F32)],
        scratch_shapes=[pltpu.VMEM((gpb, d_state, gw), F32), pltpu.VMEM((SUBLANES, xw), F32),
                        pltpu.VMEM((SUBLANES, bw), F32), pltpu.VMEM((SUBLANES, bw), F32)],
        compiler_params=_cp("parallel", "parallel", "arbitrary"),
        name="ssd_scan",
    )(xbc, xbc, xbc, state8, state8, state8, conv_w, conv_w, conv_w, conv_b, conv_b, conv_b,
      z, dt_raw, dtb, a, d_e, ng, h0)
    return y, h_new


def _lanes(v, w):
    return jnp.concatenate([v] * (w // LANES), axis=1) if w >= LANES else v[:, :w]


def _softmax_step(s, m_ref, l_ref, acc_ref, vb):
    m_old = m_ref[...]
    m_new = jnp.maximum(m_old, jnp.max(s, axis=-1, keepdims=True))
    p = jnp.exp2(s - _lanes(m_new, s.shape[1]))
    corr = jnp.exp2(m_old - m_new)
    l_ref[...] = corr * l_ref[...] + jnp.sum(p, axis=-1, keepdims=True)
    acc_ref[...] = _lanes(corr, acc_ref.shape[1]) * acc_ref[...] + _dot(p.astype(BF16), vb)
    m_ref[...] = m_new


def _diff_finish(lam, lam_init, g_ref, o_ref, l0, a0, l1, a1):
    w = a0.shape[1]
    o = a0[...] / _lanes(l0[...], w) - lam * (a1[...] / _lanes(l1[...], w))
    ms = jnp.mean(o * o, axis=-1, keepdims=True)
    o_ref[...] = (o * lax.rsqrt(ms + RMS_EPS) * g_ref[...] * (1.0 - lam_init)).astype(o_ref.dtype)


def _diff_init(m0, l0, a0, m1, l1, a1):
    for m, l, a in ((m0, l0, a0), (m1, l1, a1)):
        m[...] = jnp.full_like(m, -jnp.inf)
        l[...] = jnp.zeros_like(l)
        a[...] = jnp.zeros_like(a)


def _diff_prompt_kernel(lam_ref, q_ref, k_ref, v_ref, g_ref, o_ref, m0, l0, a0, m1, l1, a1,
                        *, tq, tk, hd, lam_init):
    qi = pl.program_id(2)
    _diff_init(m0, l0, a0, m1, l1, a1)
    q = q_ref[...]
    q0, q1 = q[:, :hd], q[:, hd:]
    per_q = tq // tk

    def block(ki, mask, r0=0):
        off = pl.multiple_of(ki * tk, tk)
        kb = k_ref[pl.ds(off, tk), :]
        vb = v_ref[pl.ds(off, tk), :]
        rows = pl.ds(r0, tq - r0)
        for qj, kj, m, l, a in ((q0, kb[:, :hd], m0, l0, a0), (q1, kb[:, hd:], m1, l1, a1)):
            s = _dot_nt(qj[r0:], kj)
            if mask is not None:
                s = jnp.where(mask, s, -jnp.inf)
            _softmax_step(s, m.at[rows], l.at[rows], a.at[rows], vb)

    def body(ki, carry):
        block(ki, None)
        return carry

    lax.fori_loop(0, qi * per_q, body, 0)
    for d in range(per_q):
        r0 = d * tk
        rc = (lax.broadcasted_iota(jnp.int32, (tq - r0, tk), 0) + r0) // MASK_CHUNK
        cc = (lax.broadcasted_iota(jnp.int32, (tq - r0, tk), 1) + r0) // MASK_CHUNK
        block(qi * per_q + d, cc <= rc, r0)
    _diff_finish(lam_ref[0], lam_init, g_ref, o_ref, l0, a0, l1, a1)


def diff_attn_prompt(q, k, v, lam, g, n_seq, seq, heads, lam_init):
    n, d = q.shape
    hw = d // heads
    hd = hw // 2
    tq = _tile(seq, TQ_DIFF, MASK_CHUNK)
    tk = _tile(tq, TK_DIFF, MASK_CHUNK)
    qb = seq // tq
    stat = pltpu.VMEM((tq, LANES), F32)
    acc = pltpu.VMEM((tq, hw), F32)
    return pl.pallas_call(
        functools.partial(_diff_prompt_kernel, tq=tq, tk=tk, hd=hd, lam_init=lam_init),
        grid=(n_seq, heads, qb),
        in_specs=[pl.BlockSpec(memory_space=pltpu.SMEM),
                  pl.BlockSpec((tq, hw), lambda b, h, i: (b * qb + i, h)),
                  pl.BlockSpec((seq, hw), lambda b, h, i: (b, h)),
                  pl.BlockSpec((seq, hw), lambda b, h, i: (b, h)),
                  pl.BlockSpec((1, hw), lambda b, h, i: (0, 0))],
        out_specs=pl.BlockSpec((tq, hw), lambda b, h, i: (b * qb + i, h)),
        out_shape=jax.ShapeDtypeStruct((n, d), BF16),
        scratch_shapes=[stat, stat, acc, stat, stat, acc],
        compiler_params=_cp("parallel", "parallel", "arbitrary"),
        name="diff_attn_prompt",
    )(lam, q, k, v, g.reshape(1, hw))


def _diff_sample_kernel(lam_ref, q_ref, kc_ref, vc_ref, kn_ref, vn_ref, g_ref, o_ref, m_ref, l_ref, a_ref,
                        *, heads, nkc, hd, lam_init):
    c = pl.program_id(1)
    hw = 2 * hd
    halves = hw // LANES

    @pl.when(c == 0)
    def _():
        m_ref[...] = jnp.full_like(m_ref, -jnp.inf)
        l_ref[...] = jnp.zeros_like(l_ref)
        a_ref[...] = jnp.zeros_like(a_ref)

    def step(h, j, kb, vb):
        i = 2 * h + j
        qj = q_ref[:, i * hd:(i + 1) * hd]
        _softmax_step(_dot_nt(qj, kb), m_ref.at[i], l_ref.at[i], a_ref.at[i], vb)

    @pl.when(c < nkc)
    def _():
        for h in range(heads):
            vb = jnp.concatenate([vc_ref[:, e * heads + h, :] for e in range(halves)], axis=1).astype(BF16)
            for j in (0, 1):
                step(h, j, kc_ref[:, 2 * h + j, :].astype(BF16), vb)

    @pl.when(c == nkc)
    def _():
        for h in range(heads):
            vb = vn_ref[:, h * hw:(h + 1) * hw]
            for j in (0, 1):
                step(h, j, kn_ref[:, (2 * h + j) * hd:(2 * h + j + 1) * hd], vb)
            _diff_finish(lam_ref[0], lam_init, g_ref, o_ref.at[:, h * hw:(h + 1) * hw],
                         l_ref.at[2 * h], a_ref.at[2 * h], l_ref.at[2 * h + 1], a_ref.at[2 * h + 1])


def diff_attn_sample(q, k_new, v_new, k_cache, v_cache, cache_off, past, lam, g, n_seq, seq, heads, lam_init):
    n, d = q.shape
    hw = d // heads
    hd = hw // 2
    nch = d // LANES
    tk = _tile(past, TK_DIFF, SUBLANES)
    nkc = past // tk
    rows = pl.BlockSpec((seq, d), lambda b, c: (b, 0))
    cache = pl.BlockSpec((tk, nch, LANES), lambda b, c: ((cache_off + b) * nkc + jnp.minimum(c, nkc - 1), 0, 0))
    return pl.pallas_call(
        functools.partial(_diff_sample_kernel, heads=heads, nkc=nkc, hd=hd, lam_init=lam_init),
        grid=(n_seq, nkc + 1),
        in_specs=[pl.BlockSpec(memory_space=pltpu.SMEM), rows, cache, cache, rows, rows,
                  pl.BlockSpec((1, hw), lambda b, c: (0, 0))],
        out_specs=rows,
        out_shape=jax.ShapeDtypeStruct((n, d), BF16),
        scratch_shapes=[pltpu.VMEM((2 * heads, seq, LANES), F32), pltpu.VMEM((2 * heads, seq, LANES), F32),
                        pltpu.VMEM((2 * heads, seq, hw), F32)],
        compiler_params=_cp("parallel", "arbitrary"),
        name="diff_attn_sample",
    )(lam, q, k_cache, v_cache, k_new, v_new, g.reshape(1, hw))


def _mem_attn_kernel(q_ref, k_ref, v_ref, o_ref, *, heads, hd, scale):
    for h in range(heads):
        sl = slice(h * hd, (h + 1) * hd)
        s = _dot_nt(q_ref[:, sl], k_ref[:, sl].astype(BF16)) * scale
        p = jnp.exp(s - jnp.max(s, axis=-1, keepdims=True))
        l = jnp.sum(p, axis=-1, keepdims=True)
        o = _dot(p.astype(BF16), v_ref[:, sl].astype(BF16)) / l
        o_ref[:, sl] = o.astype(o_ref.dtype)


def mem_attn(q, mk, mv, kv_off, m, n_seq, seq, heads):
    n, d = q.shape
    hd = d // heads
    tq = _tile(seq, TQ_MEM, 16)
    qb = seq // tq
    kv = pl.BlockSpec((m, d), lambda b, i: (kv_off + b, 0))
    return pl.pallas_call(
        functools.partial(_mem_attn_kernel, heads=heads, hd=hd, scale=hd ** -0.5),
        grid=(n_seq, qb),
        in_specs=[pl.BlockSpec((tq, d), lambda b, i: (b * qb + i, 0)), kv, kv],
        out_specs=pl.BlockSpec((tq, d), lambda b, i: (b * qb + i, 0)),
        out_shape=jax.ShapeDtypeStruct((n, d), BF16),
        compiler_params=_cp("parallel", "parallel"),
        name="mem_attn",
    )(q, mk, mv)


def _top_rows(v, k):
    rows = []
    for _ in range(k):
        m = jnp.max(v, axis=0, keepdims=True)
        rows.append(m)
        v = jnp.where(v == m, -jnp.inf, v)
    return rows


def _sort_network(lo, hi):
    def merge(lo, hi, r):
        step = r * 2
        if step < hi - lo:
            yield from merge(lo, hi, step)
            yield from merge(lo + r, hi, step)
            yield from [(i, i + r) for i in range(lo + r, hi - r, step)]
        else:
            yield (lo, lo + r)

    if hi - lo >= 1:
        mid = lo + (hi - lo) // 2
        yield from _sort_network(lo, mid)
        yield from _sort_network(mid + 1, hi)
        yield from merge(lo, hi, 1)


def _top_sorted(v):
    n = v.shape[0] // SUBLANES
    x = [v[r * SUBLANES:(r + 1) * SUBLANES] for r in range(n)]

    def order(i, j):
        x[i], x[j] = jnp.maximum(x[i], x[j]), jnp.minimum(x[i], x[j])

    for i, j in _sort_network(0, n - 1):
        order(i, j)
    shift = SUBLANES // 2
    while shift >= 1:
        x = [jnp.maximum(x[r], pltpu.roll(x[n - 1 - r], shift, 0)) for r in range(n)]
        d = n // 2
        while d >= 1:
            for i in range(n):
                if i & d == 0:
                    order(i, i + d)
            d //= 2
        shift //= 2
    return [xr[0:1] for xr in x]


def _peer_prep_kernel(xt_ref, wq_ref, keys_ref, s0_ref, e0_ref, s1_ref, e1_ref, tau_ref, *, heads, gpc):
    k = PEER_TOPK
    qt = _dot(wq_ref[...], xt_ref[...]).astype(BF16)
    half = keys_ref.shape[2]
    for h in range(heads):
        sc = [_dot(keys_ref[2 * h + j], qt[(2 * h + j) * half:(2 * h + j + 1) * half]) for j in (0, 1)]
        a = _top_sorted(sc[0])
        b = jnp.concatenate(_top_sorted(sc[1]), axis=0)
        cand = [a[0] + b] + [a[r] + b[:k // 2] for r in range(1, k // 2)]
        cand.append(jnp.concatenate(a[k // 2:], axis=0) + b[0:1])
        cand = jnp.concatenate(cand, axis=0)
        tau = _top_rows(cand, k)[-1]
        mx = a[0] + b[0:1]
        z = jnp.sum(jnp.where(cand >= tau, jnp.exp(cand - mx), 0.0), axis=0, keepdims=True)
        e0 = jnp.exp(sc[0] - a[0])
        for c in range(s0_ref.shape[1]):
            s0_ref[h, c] = sc[0][c * gpc:(c + 1) * gpc]
            e0_ref[h, c] = e0[c * gpc:(c + 1) * gpc]
        s1_ref[h] = sc[1]
        e1_ref[h] = jnp.exp(sc[1] - b[0:1]) / z
        tau_ref[h:h + 1, :] = tau


def peer_prep(xt, wq_t, keys, gpc):
    d, n = xt.shape
    hj, nk, half = keys.shape
    heads = hj // 2
    assert PEER_TOPK % 2 == 0 and nk % gpc == 0 and nk == SUBLANES * PEER_TOPK
    t = _tile(n, T_PEER, LANES)
    first = pl.BlockSpec((heads, nk // gpc, gpc, t), lambda i: (0, 0, 0, i))
    second = pl.BlockSpec((heads, nk, t), lambda i: (0, 0, i))
    f_shape = jax.ShapeDtypeStruct((heads, nk // gpc, gpc, n), F32)
    s_shape = jax.ShapeDtypeStruct((heads, nk, n), F32)
    return pl.pallas_call(
        functools.partial(_peer_prep_kernel, heads=heads, gpc=gpc),
        grid=(n // t,),
        in_specs=[pl.BlockSpec((d, t), lambda i: (0, i)),
                  pl.BlockSpec(wq_t.shape, lambda i: (0, 0)),
                  pl.BlockSpec(keys.shape, lambda i: (0, 0, 0))],
        out_specs=[first, first, second, second, pl.BlockSpec((heads, t), lambda i: (0, i))],
        out_shape=[f_shape, f_shape, s_shape, s_shape, jax.ShapeDtypeStruct((heads, n), F32)],
        compiler_params=_cp("parallel"),
        name="peer_prep",
    )(xt, wq_t, keys)


def _gelu_tanh(x):
    return 0.5 * x * (1.0 + jnp.tanh(math.sqrt(2.0 / math.pi) * (x + 0.044715 * (x * x * x))))


def _peer_mix(s0_ref, e0_ref, s1_ref, e1_ref, tau_ref, act_ref, h_ref, *, heads, nk, t):
    gpc = act_ref.shape[0] // nk
    jh = PEER_J_ROWS
    for tl in range(t // LANES):
        ls = slice(tl * LANES, (tl + 1) * LANES)
        for jb in range(nk // jh):
            js = slice(jb * jh, (jb + 1) * jh)
            gates = [jnp.zeros((jh, LANES), F32) for _ in range(gpc)]
            for h in range(heads):
                s1, e1, tau = s1_ref[h, js, ls], e1_ref[h, js, ls], tau_ref[h:h + 1, ls]
                for ii in range(gpc):
                    hit = s0_ref[h, 0, ii:ii + 1, ls] + s1 >= tau
                    gates[ii] = gates[ii] + jnp.where(hit, e0_ref[h, 0, ii:ii + 1, ls] * e1, 0.0)
            for ii in range(gpc):
                rows = slice(ii * nk + jb * jh, ii * nk + (jb + 1) * jh)
                h_ref[rows, ls] = (gates[ii] * _gelu_tanh(act_ref[rows, ls])).astype(BF16)


def _peer_main_kernel(xt_ref, u_ref, vt_ref, s0a_ref, e0a_ref, s0b_ref, e0b_ref, s1_ref, e1_ref, tau_ref,
                      x_ref, lg_ref, lb_ref, o_ref, acc_ref, act_a, act_b, h_a, h_b,
                      *, heads, nk, ns, t, alpha):
    g = pl.program_id(1)
    ec = act_a.shape[0]
    mix = functools.partial(_peer_mix, s1_ref=s1_ref, e1_ref=e1_ref, tau_ref=tau_ref, heads=heads, nk=nk, t=t)

    @pl.when(g == 0)
    def _():
        acc_ref[...] = jnp.zeros_like(acc_ref)
        act_b[...] = jnp.zeros_like(act_b)
        h_a[...] = jnp.zeros_like(h_a)
        h_b[...] = jnp.zeros_like(h_b)

    xt = xt_ref[...]
    out_a = _dot(vt_ref[:, :ec], h_a[...])
    mix(s0b_ref, e0b_ref, act_ref=act_b, h_ref=h_b)
    act_a[...] = _dot(u_ref[:ec, :], xt)
    out_b = _dot(vt_ref[:, ec:], h_b[...])
    mix(s0a_ref, e0a_ref, act_ref=act_a, h_ref=h_a)
    act_b[...] = _dot(u_ref[ec:, :], xt)
    acc_ref[...] += out_a + out_b

    @pl.when(g == ns - 1)
    def _():
        o_ref[...] = _layer_norm(alpha * x_ref[...] + acc_ref[...].T, lg_ref[...], lb_ref[...])


def peer_main(xt, u, vt, s0, e0, s1, e1, tau, x, ln_g, ln_b, alpha):
    d, n = xt.shape
    heads, nchunk, gpc, _ = s0.shape
    nk = s1.shape[1]
    n_exp = u.shape[0]
    ec = gpc * nk
    assert n_exp == nchunk * ec and nchunk % 2 == 0
    t = _tile(n, T_PEER, LANES)
    ns = nchunk // 2 + 1
    last = nchunk // 2 - 1
    first_a = pl.BlockSpec((heads, 1, gpc, t), lambda i, g: (0, jnp.minimum(2 * g, nchunk - 1), 0, i))
    first_b = pl.BlockSpec((heads, 1, gpc, t), lambda i, g: (0, jnp.maximum(2 * g - 1, 0), 0, i))
    once = pl.Buffered(1)
    second = pl.BlockSpec((heads, nk, t), lambda i, g: (0, 0, i), pipeline_mode=once)
    return pl.pallas_call(
        functools.partial(_peer_main_kernel, heads=heads, nk=nk, ns=ns, t=t, alpha=alpha),
        grid=(n // t, ns),
        in_specs=[pl.BlockSpec((d, t), lambda i, g: (0, i)),
                  pl.BlockSpec((2 * ec, d), lambda i, g: (jnp.minimum(g, last), 0)),
                  pl.BlockSpec((d, 2 * ec), lambda i, g: (0, jnp.maximum(g - 1, 0))),
                  first_a, first_a, first_b, first_b, second, second,
                  pl.BlockSpec((heads, t), lambda i, g: (0, i)),
                  pl.BlockSpec((t, d), lambda i, g: (i, 0), pipeline_mode=once),
                  pl.BlockSpec((1, d), lambda i, g: (0, 0)),
                  pl.BlockSpec((1, d), lambda i, g: (0, 0))],
        out_specs=pl.BlockSpec((t, d), lambda i, g: (i, 0)),
        out_shape=jax.ShapeDtypeStruct((n, d), F32),
        scratch_shapes=[pltpu.VMEM((d, t), F32), pltpu.VMEM((ec, t), F32), pltpu.VMEM((ec, t), F32),
                        pltpu.VMEM((ec, t), BF16), pltpu.VMEM((ec, t), BF16)],
        compiler_params=_cp("parallel", "arbitrary"),
        name="peer_main",
    )(xt, u, vt, s0, e0, s0, e0, s1, e1, tau, x, ln_g.reshape(1, d), ln_b.reshape(1, d))


def _diff_lambda(lam_vecs, layer_idx):
    lam_init = 0.8 - 0.6 * math.exp(-0.3 * layer_idx)
    lv = lam_vecs.astype(F32)
    lam = jnp.exp(jnp.sum(lv[0] * lv[1])) - jnp.exp(jnp.sum(lv[2] * lv[3])) + lam_init
    return lam.reshape(1), lam_init


def kernel(x_prompt, x_sample, state_ssd_conv, state_ssd_h, cache_diff_k, cache_diff_v, cache_mem_k, cache_mem_v, mem_prompt, ln_g, ln_b, ssd_w_in, ssd_conv_w, ssd_conv_b, ssd_a_log, ssd_dt_bias, ssd_d, ssd_norm_g, ssd_w_out, diff_w_qkv, diff_lam, diff_subln_g, diff_w_o, mem_w_q, mem_w_kv, mem_w_o, peer_w_q, peer_keys, peer_u, peer_v):
    bp, lp, d = x_prompt.shape
    bs, ls, _ = x_sample.shape
    depth = ln_g.shape[0]
    alpha = (2.0 * depth) ** 0.25
    n_mixers = 2
    ssd_heads = ssd_a_log.shape[1]
    d_inner = ssd_norm_g.shape[1]
    conv_dim = ssd_conv_w.shape[2]
    d_state = state_ssd_h.shape[-1]
    groups = (conv_dim - d_inner) // (2 * d_state)
    hpg = ssd_heads // groups
    hd_ssd = d_inner // ssd_heads
    diff_heads = cache_diff_k.shape[3]
    mem_tokens, mem_heads = cache_mem_k.shape[2], cache_mem_k.shape[3]
    peer_heads, _, n_keys, peer_half = peer_keys.shape[1:]

    streams = [
        dict(x=x_prompt.reshape(bp * lp, d), n_seq=bp, seq=lp),
        dict(x=x_sample.reshape(bs * ls, d), n_seq=bs, seq=ls),
    ]
    outs = [dict(conv=[], h=[], dk=[], dv=[]), dict(conv=[], h=[], dk=[], dv=[])]
    mk_p, mv_p = [], []

    def group_lanes(v):
        v = v.reshape(v.shape[:-1] + (groups, hpg))
        v = jnp.pad(v, [(0, 0)] * (v.ndim - 1) + [(0, LANES - hpg)])
        return v.reshape(v.shape[:-2] + (groups * LANES,))

    for i in range(depth):
        j = i // n_mixers
        g0, b0 = ln_g[i, 0], ln_b[i, 0]
        if i % n_mixers == 0:
            w_in = ssd_w_in[j]
            w_z = w_in[:, :d_inner].astype(BF16)
            w_xbc = w_in[:, d_inner:d_inner + conv_dim].astype(BF16)
            w_dt = group_lanes(w_in[:, d_inner + conv_dim:]).astype(BF16)
            dtb = group_lanes(ssd_dt_bias[j]).reshape(1, -1)
            a_neg = group_lanes(-jnp.exp(ssd_a_log[j].astype(F32))).reshape(1, -1)
            d_e = jnp.repeat(ssd_d[j], hd_ssd).reshape(1, d_inner)
            ng = ssd_norm_g[j].reshape(1, d_inner)
            w_out = ssd_w_out[j].astype(BF16)
            for si, st in enumerate(streams):
                x, n_seq, seq = st["x"], st["n_seq"], st["seq"]
                z = matmul(x, w_z)
                xbc = matmul(x, w_xbc)
                dt_raw = matmul(x, w_dt)
                if si == 0:
                    state8 = jnp.zeros((n_seq, SUBLANES, conv_dim), F32)
                    h0, h0_off = jnp.zeros((n_seq, ssd_heads, hd_ssd, d_state), F32), 0
                else:
                    state8 = jnp.pad(state_ssd_conv[j], ((0, 0), (SUBLANES - (CONV_W - 1), 0), (0, 0)))
                    h0, h0_off = state_ssd_h.reshape((-1,) + state_ssd_h.shape[2:]), j * n_seq
                outs[si]["conv"].append(xbc.reshape(n_seq, seq, conv_dim)[:, seq - (CONV_W - 1):])
                y, h_new = ssd_scan(xbc, state8, ssd_conv_w[j], ssd_conv_b[j].reshape(1, conv_dim), z, dt_raw,
                                    dtb, a_neg, d_e, ng, h0, h0_off, n_seq, seq, d_inner, d_state, groups)
                outs[si]["h"].append(h_new)
                st["x"] = matmul_ln(y, w_out, x, g0, b0, alpha)
        else:
            lam, lam_init = _diff_lambda(diff_lam[j], i)
            w_qkv = diff_w_qkv[j]
            w_q, w_k, w_v = (w_qkv[:, c * d:(c + 1) * d].astype(BF16) for c in range(3))
            w_o = diff_w_o[j].astype(BF16)
            for si, st in enumerate(streams):
                x, n_seq, seq = st["x"], st["n_seq"], st["seq"]
                hd_diff = d // diff_heads // 2
                q = matmul(x, w_q, (BF16,), scale=hd_diff ** -0.5 * math.log2(math.e))
                hw = d // diff_heads
                halves = hw // LANES
                k32, k16 = matmul_rows(x, w_k, range(d // LANES))
                v32, v16 = matmul_rows(x, w_v, [(c % halves) * diff_heads + c // halves for c in range(d // LANES)])
                outs[si]["dk"].append(k32.reshape(n_seq, seq, diff_heads, 2, hw // 2))
                outs[si]["dv"].append(v32.reshape(n_seq, seq, halves, diff_heads, LANES)
                                      .transpose(0, 1, 3, 2, 4).reshape(n_seq, seq, diff_heads, hw))
                if si == 0:
                    o = diff_attn_prompt(q, k16, v16, lam, diff_subln_g[j], n_seq, seq, diff_heads, lam_init)
                else:
                    past = cache_diff_k.shape[2]
                    kc = cache_diff_k.reshape(-1, d // LANES, LANES)
                    vc = (cache_diff_v.reshape(-1, diff_heads, halves, LANES).transpose(0, 2, 1, 3)
                          .reshape(-1, d // LANES, LANES))
                    o = diff_attn_sample(q, k16, v16, kc, vc, j * n_seq, past, lam, diff_subln_g[j],
                                         n_seq, seq, diff_heads, lam_init)
                st["x"] = matmul_ln(o, w_o, x, g0, b0, alpha)

        w_mq = mem_w_q[i].astype(BF16)
        w_mk = mem_w_kv[i][:, :d].astype(BF16)
        w_mv = mem_w_kv[i][:, d:].astype(BF16)
        w_mo = mem_w_o[i].astype(BF16)
        mem_rows = mem_prompt.reshape(bp * mem_tokens, d)
        mk = matmul(mem_rows, w_mk)
        mv = matmul(mem_rows, w_mv)
        mk_p.append(mk.reshape(bp, mem_tokens, mem_heads, d // mem_heads))
        mv_p.append(mv.reshape(bp, mem_tokens, mem_heads, d // mem_heads))
        wq_t = peer_w_q[i].T.astype(BF16)
        keys = peer_keys[i].reshape(peer_heads * 2, n_keys, peer_half).astype(BF16)
        u = peer_u[i].astype(BF16)
        vt = peer_v[i].T.astype(BF16)
        for si, st in enumerate(streams):
            x, n_seq, seq = st["x"], st["n_seq"], st["seq"]
            q = matmul(x, w_mq, (BF16,))
            if si == 0:
                kk, vv, kv_off = mk, mv, 0
            else:
                kk, vv, kv_off = cache_mem_k.reshape(-1, d), cache_mem_v.reshape(-1, d), i * n_seq
            o = mem_attn(q, kk, vv, kv_off, mem_tokens, n_seq, seq, mem_heads)
            x, xt = matmul_ln(o, w_mo, x, ln_g[i, 1], ln_b[i, 1], alpha, with_t=True)
            s0, e0, s1, e1, tau = peer_prep(xt, wq_t, keys, EC_PEER // n_keys)
            st["x"] = peer_main(xt, u, vt, s0, e0, s1, e1, tau, x, ln_g[i, 2], ln_b[i, 2], alpha)

    yp = streams[0]["x"].reshape(bp, lp, d)
    ys = streams[1]["x"].reshape(bs, ls, d)
    o_p, o_s = outs
    return (yp, ys, jnp.stack(o_p["conv"]), jnp.stack(o_p["h"]), jnp.stack(o_p["dk"]), jnp.stack(o_p["dv"]),
            jnp.stack(mk_p), jnp.stack(mv_p), jnp.stack(o_s["conv"]), jnp.stack(o_s["h"]),
            jnp.stack(o_s["dk"]), jnp.stack(o_s["dv"]))
```
